```python
import jax, jax.numpy as jnp
from jax import lax
import numpy as np

D_MODEL = 4096
BATCH = 32
SEQ = 256
DEPTH = 4
DEC_BATCH = 4
DEC_SEQ = 4096
PAST_LEN = 256

GRID_W = 64
N_MIXERS = 3
N_HEADS = 32
N_KV_HEADS = 8
HEAD_DIM = D_MODEL // N_HEADS
Q_BLOCK = 128
ROPE_THETA = 10000.0
CONV_WIDTH = 31
CONV_PAD = CONV_WIDTH // 2
N_FOURIER_GROUPS = 8
FOURIER_GROUP = D_MODEL // N_FOURIER_GROUPS
D_FF = ((8 * D_MODEL // 3 + 255) // 256) * 256
N_EXPERTS = 8
TOP_K = 2
D_FF_EXPERT = D_MODEL // 2
EPS = 1e-6
N_ATTN = (DEPTH + 2) // 3
N_CONV = (DEPTH + 1) // 3
N_FOURIER = DEPTH // 3
N_DENSE = (DEPTH + 1) // 2
N_MOE = DEPTH // 2

kernel_name = "hybrid_diffusion_attn_conv_fnet_moe_step"


def rmsnorm(x, g):
    xf = x.astype(jnp.float32)
    y = xf * lax.rsqrt(jnp.mean(xf * xf, axis=-1, keepdims=True) + EPS)
    return (y * g.astype(jnp.float32)).astype(x.dtype)


def layernorm(x, g, b):
    xf = x.astype(jnp.float32)
    mu = jnp.mean(xf, axis=-1, keepdims=True)
    var = jnp.mean(jnp.square(xf - mu), axis=-1, keepdims=True)
    y = (xf - mu) * lax.rsqrt(var + EPS)
    return (y * g.astype(jnp.float32) + b.astype(jnp.float32)).astype(x.dtype)


def adaln(cvec, w, b):
    m = jax.nn.silu(cvec) @ w + b
    return jnp.split(m[:, None, :], 6, axis=-1)


def modulate(h, shift, scale):
    return h * (1.0 + scale) + shift


def grid_positions(n_tokens):
    rows = n_tokens // GRID_W
    r, col = jnp.meshgrid(jnp.arange(rows), jnp.arange(GRID_W), indexing="ij")
    return r.reshape(-1).astype(jnp.float32), col.reshape(-1).astype(jnp.float32)


def rope_2d(x, rows, cols):
    half = HEAD_DIM // 2
    inv = ROPE_THETA ** (-jnp.arange(0, half, 2, dtype=jnp.float32) / half)

    def rot(xh, pos):
        ang = pos[:, None] * inv
        cos = jnp.cos(ang)[:, None, :]
        sin = jnp.sin(ang)[:, None, :]
        x1, x2 = jnp.split(xh.astype(jnp.float32), 2, axis=-1)
        return jnp.concatenate([x1 * cos - x2 * sin, x2 * cos + x1 * sin], axis=-1)

    out = jnp.concatenate([rot(x[..., :half], rows), rot(x[..., half:], cols)], axis=-1)
    return out.astype(x.dtype)


def qkv_heads(h, wq, wk, wv, qn, kn):
    B, T, _ = h.shape
    q = rmsnorm((h @ wq).reshape(B, T, N_HEADS, HEAD_DIM), qn)
    k = rmsnorm((h @ wk).reshape(B, T, N_KV_HEADS, HEAD_DIM), kn)
    v = (h @ wv).reshape(B, T, N_KV_HEADS, HEAD_DIM)
    return q, k, v


def blocked_attention(q, k, v):
    B, T, H, Dh = q.shape
    G = H // N_KV_HEADS
    nb = T // Q_BLOCK
    qb = q.reshape(B, nb, Q_BLOCK, N_KV_HEADS, G, Dh).transpose(1, 0, 2, 3, 4, 5)
    scale = Dh ** -0.5

    def one_block(qi):
        s = jnp.einsum("bqkgd,bskd->bkgqs", qi, k).astype(jnp.float32) * scale
        p = jax.nn.softmax(s, axis=-1).astype(v.dtype)
        return jnp.einsum("bkgqs,bskd->bqkgd", p, v)

    out = lax.map(one_block, qb)
    return out.transpose(1, 0, 2, 3, 4, 5).reshape(B, T, H * Dh)


def attn_context(h, wq, wk, wv, wo, qn, kn):
    q, k, v = qkv_heads(h, wq, wk, wv, qn, kn)
    return blocked_attention(q, k, v) @ wo, k, v


def attn_latent(h, ck, cv, rows, cols, wq, wk, wv, wo, qn, kn):
    q, k, v = qkv_heads(h, wq, wk, wv, qn, kn)
    q = rope_2d(q, rows, cols)
    k = rope_2d(k, rows, cols)
    k_all = jnp.concatenate([ck.astype(k.dtype), k], axis=1)
    v_all = jnp.concatenate([cv.astype(v.dtype), v], axis=1)
    return blocked_attention(q, k_all, v_all) @ wo


def conformer_conv(h, pw1, pw1_b, dw, dw_b, ln_g, ln_b, pw2, pw2_b):
    a = h @ pw1 + pw1_b
    u = a[..., :D_MODEL] * jax.nn.sigmoid(a[..., D_MODEL:])
    u = lax.conv_general_dilated(u, dw[:, None, :].astype(u.dtype), window_strides=(1,),
                                 padding=[(CONV_PAD, CONV_PAD)],
                                 dimension_numbers=("NWC", "WIO", "NWC"),
                                 feature_group_count=D_MODEL) + dw_b
    u = jax.nn.silu(layernorm(u, ln_g, ln_b))
    return u @ pw2 + pw2_b


def fourier_mix(h, w, b):
    B, T, D = h.shape
    hg = h.astype(jnp.float32).reshape(B, T, N_FOURIER_GROUPS, FOURIER_GROUP)
    f = jnp.fft.fftn(hg, axes=(1, 3), norm="ortho").real
    return f.reshape(B, T, D).astype(h.dtype) @ w + b


def swiglu(h, w1, w3, w2):
    return (jax.nn.silu(h @ w1) * (h @ w3)) @ w2


def moe_swiglu(h, router_w, w1, w3, w2):
    B, T, D = h.shape
    hf = h.reshape(B * T, D)
    logits = (hf @ router_w).astype(jnp.float32)
    top_v, top_i = lax.top_k(logits, TOP_K)
    top_g = jax.nn.softmax(top_v, axis=-1)
    gate = jnp.einsum("nk,nke->ne", top_g,
                      jax.nn.one_hot(top_i, N_EXPERTS, dtype=jnp.float32)).astype(h.dtype)
    out = jnp.zeros_like(hf)
    for e in range(N_EXPERTS):
        out = out + gate[:, e:e + 1] * swiglu(hf, w1[e], w3[e], w2[e])
    return out.reshape(B, T, D)


def setup_inputs(seed: int = 0) -> dict:
    key = jax.random.key(seed)
    keys = jax.random.split(key, 40)
    counter = iter(range(40))
    D = D_MODEL

    def nrm(shape, scale):
        return jax.random.normal(keys[next(counter)], shape, jnp.float32) * scale

    def gain(shape):
        return 1.0 + 0.02 * jax.random.normal(keys[next(counter)], shape, jnp.float32)

    return {
        "x_prompt": nrm((BATCH, SEQ, D), 1.0),
        "x_sample": nrm((DEC_BATCH, DEC_SEQ, D), 1.0),
        "cache_k": nrm((DEC_BATCH, N_ATTN, PAST_LEN, N_KV_HEADS, HEAD_DIM), 1.0),
        "cache_v": nrm((DEC_BATCH, N_ATTN, PAST_LEN, N_KV_HEADS, HEAD_DIM), 1.0),
        "c": nrm((DEC_BATCH, D), 1.0),
        "c_ctx": nrm((D,), 1.0),
        "mod_w": nrm((DEPTH, D, 6 * D), 0.5 * D ** -0.5),
        "mod_b": nrm((DEPTH, 6 * D), 0.02),
        "norm_g": gain((DEPTH, 2, D)),
        "wq": nrm((N_ATTN, D, N_HEADS * HEAD_DIM), D ** -0.5),
        "wk": nrm((N_ATTN, D, N_KV_HEADS * HEAD_DIM), D ** -0.5),
        "wv": nrm((N_ATTN, D, N_KV_HEADS * HEAD_DIM), D ** -0.5),
        "wo": nrm((N_ATTN, N_HEADS * HEAD_DIM, D), (N_HEADS * HEAD_DIM) ** -0.5),
        "q_norm": gain((N_ATTN, HEAD_DIM)),
        "k_norm": gain((N_ATTN, HEAD_DIM)),
        "conv_pw1": nrm((N_CONV, D, 2 * D), D ** -0.5),
        "conv_pw1_b": nrm((N_CONV, 2 * D), 0.02),
        "conv_dw": nrm((N_CONV, CONV_WIDTH, D), CONV_WIDTH ** -0.5),
        "conv_dw_b": nrm((N_CONV, D), 0.02),
        "conv_ln_g": gain((N_CONV, D)),
        "conv_ln_b": nrm((N_CONV, D), 0.02),
        "conv_pw2": nrm((N_CONV, D, D), D ** -0.5),
        "conv_pw2_b": nrm((N_CONV, D), 0.02),
        "fnet_w": nrm((N_FOURIER, D, D), D ** -0.5),
        "fnet_b": nrm((N_FOURIER, D), 0.02),
        "ffn_w1": nrm((N_DENSE, D, D_FF), D ** -0.5),
        "ffn_w3": nrm((N_DENSE, D, D_FF), D ** -0.5),
        "ffn_w2": nrm((N_DENSE, D_FF, D), D_FF ** -0.5),
        "router_w": nrm((N_MOE, D, N_EXPERTS), D ** -0.5),
        "moe_w1": nrm((N_MOE, N_EXPERTS, D, D_FF_EXPERT), D ** -0.5),
        "moe_w3": nrm((N_MOE, N_EXPERTS, D, D_FF_EXPERT), D ** -0.5),
        "moe_w2": nrm((N_MOE, N_EXPERTS, D_FF_EXPERT, D), D_FF_EXPERT ** -0.5),
    }


def reference(x_prompt, x_sample, cache_k, cache_v, c, c_ctx, mod_w, mod_b, norm_g,
              wq, wk, wv, wo, q_norm, k_norm,
              conv_pw1, conv_pw1_b, conv_dw, conv_dw_b, conv_ln_g, conv_ln_b, conv_pw2, conv_pw2_b,
              fnet_w, fnet_b, ffn_w1, ffn_w3, ffn_w2, router_w, moe_w1, moe_w3, moe_w2):
    rows_lat, cols_lat = grid_positions(x_sample.shape[1])
    yp, ys = x_prompt, x_sample
    k_list, v_list = [], []
    for layer in range(DEPTH):
        kind = layer % N_MIXERS
        j = layer // N_MIXERS
        sh1p, sc1p, g1p, sh2p, sc2p, g2p = adaln(c_ctx[None, :].astype(yp.dtype), mod_w[layer], mod_b[layer])
        sh1s, sc1s, g1s, sh2s, sc2s, g2s = adaln(c.astype(ys.dtype), mod_w[layer], mod_b[layer])

        hp = modulate(rmsnorm(yp, norm_g[layer, 0]), sh1p, sc1p)
        hs = modulate(rmsnorm(ys, norm_g[layer, 0]), sh1s, sc1s)
        if kind == 0:
            mp, kc, vc = attn_context(hp, wq[j], wk[j], wv[j], wo[j], q_norm[j], k_norm[j])
            ms = attn_latent(hs, cache_k[:, j], cache_v[:, j], rows_lat, cols_lat,
                             wq[j], wk[j], wv[j], wo[j], q_norm[j], k_norm[j])
            k_list.append(kc)
            v_list.append(vc)
        elif kind == 1:
            cp = (conv_pw1[j], conv_pw1_b[j], conv_dw[j], conv_dw_b[j],
                  conv_ln_g[j], conv_ln_b[j], conv_pw2[j], conv_pw2_b[j])
            mp = conformer_conv(hp, *cp)
            ms = conformer_conv(hs, *cp)
        else:
            mp = fourier_mix(hp, fnet_w[j], fnet_b[j])
            ms = fourier_mix(hs, fnet_w[j], fnet_b[j])
        yp = yp + g1p * mp
        ys = ys + g1s * ms

        hp = modulate(rmsnorm(yp, norm_g[layer, 1]), sh2p, sc2p)
        hs = modulate(rmsnorm(ys, norm_g[layer, 1]), sh2s, sc2s)
        f = layer // 2
        if layer % 2 == 0:
            fp = swiglu(hp, ffn_w1[f], ffn_w3[f], ffn_w2[f])
            fs = swiglu(hs, ffn_w1[f], ffn_w3[f], ffn_w2[f])
        else:
            fp = moe_swiglu(hp, router_w[f], moe_w1[f], moe_w3[f], moe_w2[f])
            fs = moe_swiglu(hs, router_w[f], moe_w1[f], moe_w3[f], moe_w2[f])
        yp = yp + g2p * fp
        ys = ys + g2s * fs

    k_ctx = jnp.stack(k_list, axis=1)
    v_ctx = jnp.stack(v_list, axis=1)
    return (yp, ys, k_ctx, v_ctx)
```

```python
import functools
import math

import jax
import jax.numpy as jnp
from jax import lax
from jax.experimental import pallas as pl
from jax.experimental.pallas import tpu as pltpu

F32 = jnp.float32
BF16 = jnp.bfloat16

EPS = 1e-6
ROPE_THETA = 10000.0
GRID_W = 64
N_MIXERS = 3
N_FOURIER_GROUPS = 8
TOP_K = 2
LANES = 128
ROUTE_LANES = 128
VMEM_LIMIT_MB = 60


def _tile(n, pref, mult):
    t = min(pref, n)
    t -= t % mult
    while t >= mult:
        if n % t == 0:
            return t
        t -= mult
    return n


def _params(n_axes, vmem_mb=VMEM_LIMIT_MB):
    return pltpu.CompilerParams(dimension_semantics=("arbitrary",) * n_axes,
                                vmem_limit_bytes=vmem_mb * 1024 * 1024)


def _bdot(a, b):
    return jnp.dot(a.astype(BF16), b.astype(BF16), preferred_element_type=F32)


class _Stream:
    def __init__(self, batch, seq, dec_batch, dec_seq):
        self.batch, self.seq, self.dec_batch, self.dec_seq = batch, seq, dec_batch, dec_seq
        self.n_ctx = batch * seq
        self.n_lat = dec_batch * dec_seq
        self.n_tok = self.n_ctx + self.n_lat
        self.tm = _tile(math.gcd(self.n_ctx, dec_seq), 1024, 16)

    def mod_row(self, i, tm):
        r0 = i * tm
        return jnp.where(r0 < self.n_ctx, 0, 1 + (r0 - self.n_ctx) // self.dec_seq)

    def lat_tile(self, i, tm):
        t0 = self.n_ctx // tm
        return jnp.where(i >= t0, (i - t0) % (self.dec_seq // tm), 0)


def _mod_spec(st, layer, which, tm, width, col_axis):
    if col_axis:
        return pl.BlockSpec((None, None, None, 1, width),
                            lambda i, j: (layer, st.mod_row(i, tm), which, 0, j))
    return pl.BlockSpec((None, None, None, 1, width),
                        lambda i: (layer, st.mod_row(i, tm), which, 0, 0))


def _adaln_kernel(c_ref, w_ref, b_ref, o_ref):
    c = c_ref[...]
    s = c * jax.nn.sigmoid(c)
    o_ref[...] = _bdot(s, w_ref[...]) + b_ref[...]


def _adaln_table(cvec8, mod_w, mod_b):
    depth, d, n6 = mod_w.shape
    tn = _tile(n6, 1024, LANES)
    out = pl.pallas_call(
        _adaln_kernel,
        grid=(depth, n6 // tn),
        in_specs=[pl.BlockSpec((8, d), lambda l, j: (0, 0)),
                  pl.BlockSpec((None, d, tn), lambda l, j: (l, 0, j)),
                  pl.BlockSpec((None, 1, tn), lambda l, j: (l, 0, j))],
        out_specs=pl.BlockSpec((None, 8, tn), lambda l, j: (l, 0, j)),
        out_shape=jax.ShapeDtypeStruct((depth, 8, n6), F32),
        compiler_params=_params(2),
        name="adaln_table",
    )(cvec8, mod_w, mod_b.reshape(depth, 1, n6))
    return out.reshape(depth, 8, 6, 1, d)


def _norm_mod_kernel(y_ref, g_ref, sh_ref, sc_ref, o_ref):
    y = y_ref[...]
    h = y * lax.rsqrt(jnp.mean(y * y, axis=-1, keepdims=True) + EPS) * g_ref[...]
    o_ref[...] = (h * (1.0 + sc_ref[...]) + sh_ref[...]).astype(o_ref.dtype)


def _norm_mod_route_kernel(y_ref, g_ref, sh_ref, sc_ref, rw_ref, o_ref, r_ref, *, n_experts):
    y = y_ref[...]
    h = y * lax.rsqrt(jnp.mean(y * y, axis=-1, keepdims=True) + EPS) * g_ref[...]
    h = h * (1.0 + sc_ref[...]) + sh_ref[...]
    o_ref[...] = h
    logits = jnp.dot(h, rw_ref[...], preferred_element_type=F32,
                     precision=lax.Precision.HIGHEST)
    lane = lax.broadcasted_iota(jnp.int32, logits.shape, 1).astype(F32)
    neg = jnp.float32(-jnp.inf)
    lg = jnp.where(lane < n_experts, logits, neg)
    m1 = jnp.max(lg, axis=-1, keepdims=True)
    i1 = jnp.min(jnp.where(lg == m1, lane, float(ROUTE_LANES)), axis=-1, keepdims=True)
    lg2 = jnp.where(lane == i1, neg, lg)
    m2 = jnp.max(lg2, axis=-1, keepdims=True)
    i2 = jnp.min(jnp.where(lg2 == m2, lane, float(ROUTE_LANES)), axis=-1, keepdims=True)
    e = jnp.exp(m2 - m1)
    g1 = 1.0 / (1.0 + e)
    g2 = e / (1.0 + e)
    r_ref[...] = jnp.where(lane == 0, i1, jnp.where(lane == 1, i2,
                           jnp.where(lane == 2, g1, jnp.where(lane == 3, g2, 0.0))))


def _norm_mod(st, y, norm_g4, mods, layer, sub, out_dtype, router_w_pad=None, n_experts=0):
    n, d = y.shape
    tm = _tile(st.tm, 512, 16)
    in_specs = [pl.BlockSpec((tm, d), lambda i: (i, 0)),
                pl.BlockSpec((None, None, 1, d), lambda i: (layer, sub, 0, 0)),
                _mod_spec(st, layer, 3 * sub + 0, tm, d, False),
                _mod_spec(st, layer, 3 * sub + 1, tm, d, False)]
    if router_w_pad is None:
        return pl.pallas_call(
            _norm_mod_kernel, grid=(n // tm,), in_specs=in_specs,
            out_specs=pl.BlockSpec((tm, d), lambda i: (i, 0)),
            out_shape=jax.ShapeDtypeStruct((n, d), out_dtype),
            compiler_params=_params(1), name="norm_mod",
        )(y, norm_g4, mods, mods)
    in_specs.append(pl.BlockSpec((d, ROUTE_LANES), lambda i: (0, 0)))
    return pl.pallas_call(
        functools.partial(_norm_mod_route_kernel, n_experts=n_experts),
        grid=(n // tm,), in_specs=in_specs,
        out_specs=[pl.BlockSpec((tm, d), lambda i: (i, 0)),
                   pl.BlockSpec((tm, ROUTE_LANES), lambda i: (i, 0))],
        out_shape=[jax.ShapeDtypeStruct((n, d), F32),
                   jax.ShapeDtypeStruct((n, ROUTE_LANES), F32)],
        compiler_params=_params(1), name="norm_mod_route",
    )(y, norm_g4, mods, mods, router_w_pad)


def _mm_plain_kernel(x_ref, w_ref, *rest, has_bias):
    acc = _bdot(x_ref[...], w_ref[...])
    outs = rest
    if has_bias:
        acc = acc + rest[0][...]
        outs = rest[1:]
    for o_ref in outs:
        o_ref[...] = acc.astype(o_ref.dtype)


def _mm_plain(x, w3, wi, *, tm, tn, out_dtypes, bias3=None, block_diag=False):
    m = x.shape[0]
    kb, n_w = w3.shape[1], w3.shape[2]
    if block_diag:
        n_out, tn = x.shape[1], n_w
        x_spec = pl.BlockSpec((tm, kb), lambda i, j: (i, j))
        w_spec = pl.BlockSpec((None, kb, n_w), lambda i, j: (wi, 0, 0))
    else:
        n_out = n_w
        x_spec = pl.BlockSpec((tm, kb), lambda i, j: (i, 0))
        w_spec = pl.BlockSpec((None, kb, tn), lambda i, j: (wi, 0, j))
    in_specs, args = [x_spec, w_spec], [x, w3]
    if bias3 is not None:
        in_specs.append(pl.BlockSpec((None, 1, tn), lambda i, j: (wi, 0, j)))
        args.append(bias3)
    outs = pl.pallas_call(
        functools.partial(_mm_plain_kernel, has_bias=bias3 is not None),
        grid=(m // tm, n_out // tn), in_specs=in_specs,
        out_specs=[pl.BlockSpec((tm, tn), lambda i, j: (i, j)) for _ in out_dtypes],
        out_shape=[jax.ShapeDtypeStruct((m, n_out), dt) for dt in out_dtypes],
        compiler_params=_params(2), name="mm_plain",
    )(*args)
    return outs


def _mm_glu_kernel(x_ref, wa_ref, wb_ref, *rest, has_bias, act):
    x = x_ref[...]
    a = _bdot(x, wa_ref[...])
    b = _bdot(x, wb_ref[...])
    if has_bias:
        a = a + rest[0][...]
        b = b + rest[1][...]
    o_ref = rest[-1]
    if act == "swiglu":
        o_ref[...] = (a * jax.nn.sigmoid(a) * b).astype(o_ref.dtype)
    else:
        o_ref[...] = (a * jax.nn.sigmoid(b)).astype(o_ref.dtype)


def _mm_glu(x, wa3, wb3, wi, *, n_out, b_col_off, tm, tn, act, out_dtype, bias3=None):
    m, k = x.shape
    off = b_col_off // tn
    in_specs = [pl.BlockSpec((tm, k), lambda i, j: (i, 0)),
                pl.BlockSpec((None, k, tn), lambda i, j: (wi, 0, j)),
                pl.BlockSpec((None, k, tn), lambda i, j: (wi, 0, j + off))]
    args = [x, wa3, wb3]
    if bias3 is not None:
        in_specs += [pl.BlockSpec((None, 1, tn), lambda i, j: (wi, 0, j)),
                     pl.BlockSpec((None, 1, tn), lambda i, j: (wi, 0, j + off))]
        args += [bias3, bias3]
    return pl.pallas_call(
        functools.partial(_mm_glu_kernel, has_bias=bias3 is not None, act=act),
        grid=(m // tm, n_out // tn), in_specs=in_specs,
        out_specs=pl.BlockSpec((tm, tn), lambda i, j: (i, j)),
        out_shape=jax.ShapeDtypeStruct((m, n_out), out_dtype),
        compiler_params=_params(2), name="mm_glu",
    )(*args)


def _mm_resid_kernel(x_ref, w_ref, *rest, has_bias):
    acc = _bdot(x_ref[...], w_ref[...])
    if has_bias:
        acc = acc + rest[0][...]
        rest = rest[1:]
    y_ref, gate_ref, o_ref = rest
    o_ref[...] = y_ref[...] + gate_ref[...] * acc


def _mm_resid(st, x, w3, wi, y, mods, layer, which_gate, *, tm, tn, bias3=None):
    m, k = x.shape
    n = w3.shape[2]
    in_specs = [pl.BlockSpec((tm, k), lambda i, j: (i, 0)),
                pl.BlockSpec((None, k, tn), lambda i, j: (wi, 0, j))]
    args = [x, w3]
    if bias3 is not None:
        in_specs.append(pl.BlockSpec((None, 1, tn), lambda i, j: (wi, 0, j)))
        args.append(bias3)
    in_specs += [pl.BlockSpec((tm, tn), lambda i, j: (i, j)),
                 _mod_spec(st, layer, which_gate, tm, tn, True)]
    args += [y, mods]
    return pl.pallas_call(
        functools.partial(_mm_resid_kernel, has_bias=bias3 is not None),
        grid=(m // tm, n // tn), in_specs=in_specs,
        out_specs=pl.BlockSpec((tm, tn), lambda i, j: (i, j)),
        out_shape=jax.ShapeDtypeStruct((m, n), F32),
        compiler_params=_params(2), name="mm_resid",
    )(*args)


def _mm_headnorm_kernel(x_ref, w_ref, g_ref, cos_ref, sin_ref, ob_ref, *rest, first_lat_tile):
    acc = _bdot(x_ref[...], w_ref[...])
    tm, tn = acc.shape
    is_lat = pl.program_id(0) >= first_lat_tile
    lane = lax.broadcasted_iota(jnp.int32, (tm, LANES), 1)
    low = (lane % (LANES // 2)) < (LANES // 4)
    g = g_ref[...]
    c = cos_ref[...]
    s = sin_ref[...]
    for hh in range(tn // LANES):
        cols = slice(hh * LANES, (hh + 1) * LANES)
        a = acc[:, cols]
        r = a * lax.rsqrt(jnp.mean(a * a, axis=-1, keepdims=True) + EPS) * g
        if rest:
            rest[0][:, cols] = r
        partner = jnp.where(low, pltpu.roll(r, LANES - LANES // 4, 1), pltpu.roll(r, LANES // 4, 1))
        roped = r * c + partner * s
        ob_ref[:, cols] = jnp.where(is_lat, roped, r).astype(ob_ref.dtype)


def _mm_headnorm(st, x, w3, wi, g3, cos, sin, *, tm, tn, want_f32):
    m, k = x.shape
    n = w3.shape[2]
    out_specs = [pl.BlockSpec((tm, tn), lambda i, j: (i, j))]
    out_shape = [jax.ShapeDtypeStruct((m, n), BF16)]
    if want_f32:
        out_specs.append(pl.BlockSpec((tm, tn), lambda i, j: (i, j)))
        out_shape.append(jax.ShapeDtypeStruct((m, n), F32))
    return pl.pallas_call(
        functools.partial(_mm_headnorm_kernel, first_lat_tile=st.n_ctx // tm),
        grid=(m // tm, n // tn),
        in_specs=[pl.BlockSpec((tm, k), lambda i, j: (i, 0)),
                  pl.BlockSpec((None, k, tn), lambda i, j: (wi, 0, j)),
                  pl.BlockSpec((None, 1, LANES), lambda i, j: (wi, 0, 0)),
                  pl.BlockSpec((tm, LANES), lambda i, j: (st.lat_tile(i, tm), 0)),
                  pl.BlockSpec((tm, LANES), lambda i, j: (st.lat_tile(i, tm), 0))],
        out_specs=out_specs, out_shape=out_shape,
        compiler_params=_params(2), name="mm_headnorm",
    )(x, w3, g3, cos, sin)


def _rope_tables(dec_seq, head_dim):
    half = head_dim // 2
    t = jnp.arange(dec_seq)
    rows = (t // GRID_W).astype(F32)
    cols = (t % GRID_W).astype(F32)
    inv = ROPE_THETA ** (-jnp.arange(0, half, 2, dtype=F32) / half)
    ar = rows[:, None] * inv
    ac = cols[:, None] * inv
    cos = jnp.concatenate([jnp.cos(ar), jnp.cos(ar), jnp.cos(ac), jnp.cos(ac)], axis=-1)
    sin = jnp.concatenate([-jnp.sin(ar), jnp.sin(ar), -jnp.sin(ac), jnp.sin(ac)], axis=-1)
    return cos, sin


def _softmax_pv(q4, kvs, scale):
    dn = (((1,), (1,)), ((), ()))
    ss = [lax.dot_general(q4, k, dn, preferred_element_type=F32) * scale for k, _ in kvs]
    m = ss[0].max(axis=-1, keepdims=True)
    for s in ss[1:]:
        m = jnp.maximum(m, s.max(axis=-1, keepdims=True))
    ps = [jnp.exp(s - m) for s in ss]
    l = ps[0].sum(axis=-1, keepdims=True)
    for p in ps[1:]:
        l = l + p.sum(axis=-1, keepdims=True)
    o = jnp.dot(ps[0].astype(BF16), kvs[0][1], preferred_element_type=F32)
    for p, (_, v) in zip(ps[1:], kvs[1:]):
        o = o + jnp.dot(p.astype(BF16), v, preferred_element_type=F32)
    return o / l


def _attn_ctx_kernel(q_ref, k_ref, v_ref, o_ref, *, n_seq, n_kv, group, head_dim):
    @pl.when(pl.program_id(0) >= n_seq)
    def _():
        o_ref[...] = jnp.zeros_like(o_ref)

    pl.when(pl.program_id(0) < n_seq)(
        functools.partial(_attn_ctx_body, q_ref, k_ref, v_ref, o_ref, n_kv, group, head_dim))


def _attn_ctx_body(q_ref, k_ref, v_ref, o_ref, n_kv, group, head_dim):
    tq = q_ref.shape[0]
    scale = head_dim ** -0.5
    for kv in range(n_kv):
        kh = k_ref[:, kv * head_dim:(kv + 1) * head_dim]
        vh = v_ref[:, kv * head_dim:(kv + 1) * head_dim]
        c0 = kv * group * head_dim
        q4 = jnp.concatenate([q_ref[:, c0 + g * head_dim:c0 + (g + 1) * head_dim]
                              for g in range(group)], axis=0)
        o = _softmax_pv(q4, [(kh, vh)], scale)
        for g in range(group):
            o_ref[:, c0 + g * head_dim:c0 + (g + 1) * head_dim] = (
                o[g * tq:(g + 1) * tq].astype(o_ref.dtype))


def _attn_lat_kernel(q_ref, k_ref, v_ref, ck_ref, cv_ref, prev_ref, o_ref, *, group, head_dim):
    del prev_ref
    tq = q_ref.shape[0]
    scale = head_dim ** -0.5
    q4 = jnp.concatenate([q_ref[:, g * head_dim:(g + 1) * head_dim] for g in range(group)], axis=0)
    kvs = [(ck_ref[...].astype(BF16), cv_ref[...].astype(BF16)), (k_ref[...], v_ref[...])]
    o = _softmax_pv(q4, kvs, scale)
    for g in range(group):
        o_ref[:, g * head_dim:(g + 1) * head_dim] = o[g * tq:(g + 1) * tq].astype(o_ref.dtype)


def _attention(st, q, k, v, cache_k3, cache_v3, n_kv, head_dim):
    n, d = q.shape
    kvd = n_kv * head_dim
    group = d // kvd
    gw = group * head_dim
    ctx = pl.pallas_call(
        functools.partial(_attn_ctx_kernel, n_seq=st.batch, n_kv=n_kv, group=group, head_dim=head_dim),
        grid=(n // st.seq,),
        in_specs=[pl.BlockSpec((st.seq, d), lambda b: (jnp.minimum(b, st.batch - 1), 0)),
                  pl.BlockSpec((st.seq, kvd), lambda b: (jnp.minimum(b, st.batch - 1), 0)),
                  pl.BlockSpec((st.seq, kvd), lambda b: (jnp.minimum(b, st.batch - 1), 0))],
        out_specs=pl.BlockSpec((st.seq, d), lambda b: (b, 0)),
        out_shape=jax.ShapeDtypeStruct((n, d), BF16),
        compiler_params=_params(1), name="attn_ctx",
    )(q, k, v)
    tq = _tile(st.dec_seq, 128, 16)
    qt = st.dec_seq // tq
    q0 = st.n_ctx // tq
    s0 = st.n_ctx // st.dec_seq
    past = cache_k3.shape[1]
    return pl.pallas_call(
        functools.partial(_attn_lat_kernel, group=group, head_dim=head_dim),
        grid=(st.dec_batch, n_kv, qt),
        in_specs=[pl.BlockSpec((tq, gw), lambda b, h, i: (q0 + b * qt + i, h)),
                  pl.BlockSpec((st.dec_seq, head_dim), lambda b, h, i: (s0 + b, h)),
                  pl.BlockSpec((st.dec_seq, head_dim), lambda b, h, i: (s0 + b, h)),
                  pl.BlockSpec((None, past, head_dim), lambda b, h, i: (b, 0, h)),
                  pl.BlockSpec((None, past, head_dim), lambda b, h, i: (b, 0, h)),
                  pl.BlockSpec(memory_space=pl.ANY)],
        out_specs=pl.BlockSpec((tq, gw), lambda b, h, i: (q0 + b * qt + i, h)),
        out_shape=jax.ShapeDtypeStruct((n, d), BF16),
        input_output_aliases={5: 0},
        compiler_params=_params(3), name="attn_lat",
    )(q, k, v, cache_k3, cache_v3, ctx)


CONV_HALO = 16


def _conv_ln_kernel(cur_ref, prev_ref, next_ref, dw_ref, dwb_ref, g_ref, b_ref, o_ref, xp_ref, cv_ref,
                    *, st, tc, width, cw):
    i = pl.program_id(0)
    r0 = i * tc
    in_ctx = r0 < st.n_ctx
    pos = jnp.where(in_ctx, r0 % st.seq, (r0 - st.n_ctx) % st.dec_seq)
    seq_len = jnp.where(in_ctx, st.seq, st.dec_seq)
    first = pos == 0
    last = pos + tc == seq_len
    d = cur_ref.shape[1]
    xp_ref[0:CONV_HALO, :] = jnp.where(first, 0.0, prev_ref[...])
    xp_ref[CONV_HALO:CONV_HALO + tc, :] = cur_ref[...]
    xp_ref[CONV_HALO + tc:2 * CONV_HALO + tc, :] = jnp.where(last, 0.0, next_ref[...])
    base = CONV_HALO - width // 2

    def chunk(c, carry):
        off = pl.multiple_of(c * cw, cw)
        acc = jnp.zeros((tc, cw), F32)
        for j in range(width):
            acc = acc + xp_ref[pl.ds(base + j, tc), pl.ds(off, cw)] * dw_ref[pl.ds(j, 1), pl.ds(off, cw)]
        cv_ref[:, pl.ds(off, cw)] = acc + dwb_ref[:, pl.ds(off, cw)]
        return carry

    lax.fori_loop(0, d // cw, chunk, 0)
    u = cv_ref[...]
    mu = jnp.mean(u, axis=-1, keepdims=True)
    uc = u - mu
    var = jnp.mean(uc * uc, axis=-1, keepdims=True)
    yv = uc * lax.rsqrt(var + EPS) * g_ref[...] + b_ref[...]
    o_ref[...] = (yv * jax.nn.sigmoid(yv)).astype(o_ref.dtype)


def _conv_ln(st, u, dw3, dwb3, lng3, lnb3, wi):
    n, d = u.shape
    width = dw3.shape[1]
    tc = _tile(math.gcd(st.seq, st.dec_seq), 256, CONV_HALO)
    hb = tc // CONV_HALO
    n_halo = n // CONV_HALO
    cw = LANES
    vec = lambda: pl.BlockSpec((None, 1, d), lambda i: (wi, 0, 0))
    return pl.pallas_call(
        functools.partial(_conv_ln_kernel, st=st, tc=tc, width=width, cw=cw),
        grid=(n // tc,),
        in_specs=[pl.BlockSpec((tc, d), lambda i: (i, 0)),
                  pl.BlockSpec((CONV_HALO, d), lambda i: (jnp.maximum(i * hb - 1, 0), 0)),
                  pl.BlockSpec((CONV_HALO, d), lambda i: (jnp.minimum((i + 1) * hb, n_halo - 1), 0)),
                  pl.BlockSpec((None, width, d), lambda i: (wi, 0, 0)),
                  vec(), vec(), vec()],
        out_specs=pl.BlockSpec((tc, d), lambda i: (i, 0)),
        out_shape=jax.ShapeDtypeStruct((n, d), BF16),
        scratch_shapes=[pltpu.VMEM((tc + 2 * CONV_HALO, d), F32), pltpu.VMEM((tc, d), F32)],
        compiler_params=_params(1), name="conv_ln",
    )(u, u, u, dw3, dwb3, lng3, lnb3)


def _dft_mats(n):
    j = jnp.arange(n, dtype=jnp.int32)
    ang = ((j[:, None] * j[None, :]) % n).astype(F32) * (2.0 * math.pi / n)
    return jnp.cos(ang).astype(BF16), jnp.sin(ang).astype(BF16)


def _dft_time_kernel(ct_ref, st_ref, xc_ref, xs_ref, *rest, scale, n_seq):
    o_ref = rest[-1]

    @pl.when(pl.program_id(1) < n_seq)
    def _():
        acc = (jnp.dot(ct_ref[...], xc_ref[...], preferred_element_type=F32)
               - jnp.dot(st_ref[...], xs_ref[...], preferred_element_type=F32))
        o_ref[...] = (acc * scale).astype(o_ref.dtype)

    @pl.when(pl.program_id(1) >= n_seq)
    def _():
        o_ref[...] = jnp.zeros_like(o_ref)


def _dft_time(xc, xs, n_seq, t, row0, scale, prev=None):
    n, d = xc.shape
    ct, sn = _dft_mats(t)
    tr = _tile(t, 512, 16)
    tn = _tile(d, 512 if t > 1024 else 2048, LANES)
    s0 = row0 // t
    r0 = row0 // tr
    rt = t // tr
    n_steps = n_seq if prev is not None else (n - row0) // t
    in_specs = [pl.BlockSpec((tr, t), lambda i, b, j: (i, 0)),
                pl.BlockSpec((tr, t), lambda i, b, j: (i, 0)),
                pl.BlockSpec((t, tn), lambda i, b, j: (s0 + jnp.minimum(b, n_seq - 1), j)),
                pl.BlockSpec((t, tn), lambda i, b, j: (s0 + jnp.minimum(b, n_seq - 1), j))]
    args = [ct, sn, xc, xs]
    aliases = {}
    if prev is not None:
        in_specs.append(pl.BlockSpec(memory_space=pl.ANY))
        args.append(prev)
        aliases = {4: 0}
    return pl.pallas_call(
        functools.partial(_dft_time_kernel, scale=scale, n_seq=n_seq),
        grid=(rt, n_steps, d // tn), in_specs=in_specs,
        out_specs=pl.BlockSpec((tr, tn), lambda i, b, j: (r0 + b * rt + i, j)),
        out_shape=jax.ShapeDtypeStruct((n, d), BF16),
        input_output_aliases=aliases,
        compiler_params=_params(3), name="dft_time",
    )(*args)


def _fourier(st, h):
    n, d = h.shape
    gsz = d // N_FOURIER_GROUPS
    cc, sc = _dft_mats(gsz)
    tm = _tile(n, 2048, 16)
    xc, = _mm_plain(h, cc[None], 0, tm=tm, tn=gsz, out_dtypes=[BF16], block_diag=True)
    xs, = _mm_plain(h, sc[None], 0, tm=tm, tn=gsz, out_dtypes=[BF16], block_diag=True)
    f = _dft_time(xc, xs, st.batch, st.seq, 0, (st.seq * gsz) ** -0.5)
    return _dft_time(xc, xs, st.dec_batch, st.dec_seq, st.n_ctx, (st.dec_seq * gsz) ** -0.5, prev=f)


def _gather_rows_kernel(src_ref, h_ref, o_ref, buf_ref, sem):
    tg = buf_ref.shape[0]

    def row_copy(r, t):
        return pltpu.make_async_copy(h_ref.at[pl.ds(t, 1)], buf_ref.at[pl.ds(r, 1)], sem)

    def issue(r, carry):
        row_copy(r, src_ref[0, r]).start()
        return carry

    def wait(r, carry):
        row_copy(r, 0).wait()
        return carry

    lax.fori_loop(0, tg, issue, 0)
    lax.fori_loop(0, tg, wait, 0)
    o_ref[...] = buf_ref[...].astype(o_ref.dtype)


def _gather_rows(h, src, tg):
    n, d = h.shape
    p = src.shape[0]
    return pl.pallas_call(
        _gather_rows_kernel,
        grid=(p // tg,),
        in_specs=[pl.BlockSpec((None, 1, tg), lambda i: (i, 0, 0), memory_space=pltpu.SMEM),
                  pl.BlockSpec(memory_space=pl.ANY)],
        out_specs=pl.BlockSpec((tg, d), lambda i: (i, 0)),
        out_shape=jax.ShapeDtypeStruct((p, d), BF16),
        scratch_shapes=[pltpu.VMEM((tg, d), F32), pltpu.SemaphoreType.DMA(())],
        compiler_params=_params(1), name="moe_gather",
    )(src.reshape(p // tg, 1, tg), h)


def _moe_up_kernel(te_ref, tv_ref, x_ref, w1_ref, w3_ref, o_ref):
    i = pl.program_id(1)

    @pl.when(tv_ref[i] == 1)
    def _():
        x = x_ref[...]
        a = _bdot(x, w1_ref[...])
        b = _bdot(x, w3_ref[...])
        o_ref[...] = (a * jax.nn.sigmoid(a) * b).astype(o_ref.dtype)

    @pl.when(tv_ref[i] == 0)
    def _():
        o_ref[...] = jnp.zeros_like(o_ref)


def _moe_down_kernel(te_ref, tv_ref, x_ref, w_ref, o_ref):
    i = pl.program_id(1)

    @pl.when(tv_ref[i] == 1)
    def _():
        o_ref[...] = _bdot(x_ref[...], w_ref[...])

    @pl.when(tv_ref[i] == 0)
    def _():
        o_ref[...] = jnp.zeros_like(o_ref)


def _moe_experts(xs, tile_e, tile_valid, w1, w3, w2, f, tm):
    p, d = xs.shape
    ff = w1.shape[3]
    tn = _tile(ff, 512, LANES)
    g = pl.pallas_call(
        _moe_up_kernel,
        grid_spec=pltpu.PrefetchScalarGridSpec(
            num_scalar_prefetch=2, grid=(ff // tn, p // tm),
            in_specs=[pl.BlockSpec((tm, d), lambda j, i, te, tv: (i, 0)),
                      pl.BlockSpec((None, None, d, tn), lambda j, i, te, tv: (f, te[i], 0, j)),
                      pl.BlockSpec((None, None, d, tn), lambda j, i, te, tv: (f, te[i], 0, j))],
            out_specs=pl.BlockSpec((tm, tn), lambda j, i, te, tv: (i, j))),
        out_shape=jax.ShapeDtypeStruct((p, ff), BF16),
        compiler_params=_params(2), name="moe_up",
    )(tile_e, tile_valid, xs, w1, w3)
    tn2 = _tile(d, 1024, LANES)
    return pl.pallas_call(
        _moe_down_kernel,
        grid_spec=pltpu.PrefetchScalarGridSpec(
            num_scalar_prefetch=2, grid=(d // tn2, p // tm),
            in_specs=[pl.BlockSpec((tm, ff), lambda j, i, te, tv: (i, 0)),
                      pl.BlockSpec((None, None, ff, tn2), lambda j, i, te, tv: (f, te[i], 0, j))],
            out_specs=pl.BlockSpec((tm, tn2), lambda j, i, te, tv: (i, j))),
        out_shape=jax.ShapeDtypeStruct((p, d), F32),
        compiler_params=_params(2), name="moe_down",
    )(tile_e, tile_valid, g, w2)


def _moe_combine_kernel(pos_ref, route_ref, y_ref, gate_ref, ye_ref, o_ref, ya_ref, yb_ref, sem):
    tc = ya_ref.shape[0]

    def row_copy(r, k, p):
        dst = ya_ref if k == 0 else yb_ref
        return pltpu.make_async_copy(ye_ref.at[pl.ds(p, 1)], dst.at[pl.ds(r, 1)], sem)

    def issue(r, carry):
        row_copy(r, 0, pos_ref[0, 2 * r]).start()
        row_copy(r, 1, pos_ref[0, 2 * r + 1]).start()
        return carry

    def wait(r, carry):
        row_copy(r, 0, 0).wait()
        row_copy(r, 1, 0).wait()
        return carry

    lax.fori_loop(0, tc, issue, 0)
    lax.fori_loop(0, tc, wait, 0)
    route = route_ref[...]
    f = route[:, 2:3] * ya_ref[...] + route[:, 3:4] * yb_ref[...]
    o_ref[...] = y_ref[...] + gate_ref[...] * f


def _moe_combine(st, y, ye, pos, route, mods, layer, which_gate):
    n, d = y.shape
    tc = _tile(st.tm, 128, 8)
    return pl.pallas_call(
        _moe_combine_kernel,
        grid=(n // tc,),
        in_specs=[pl.BlockSpec((None, 1, TOP_K * tc), lambda i: (i, 0, 0), memory_space=pltpu.SMEM),
                  pl.BlockSpec((tc, ROUTE_LANES), lambda i: (i, 0)),
                  pl.BlockSpec((tc, d), lambda i: (i, 0)),
                  _mod_spec(st, layer, which_gate, tc, d, False),
                  pl.BlockSpec(memory_space=pl.ANY)],
        out_specs=pl.BlockSpec((tc, d), lambda i: (i, 0)),
        out_shape=jax.ShapeDtypeStruct((n, d), F32),
        scratch_shapes=[pltpu.VMEM((tc, d), F32), pltpu.VMEM((tc, d), F32),
                        pltpu.SemaphoreType.DMA(())],
        compiler_params=_params(1), name="moe_combine",
    )(pos.reshape(n // tc, 1, TOP_K * tc), route, y, mods, ye)


def _moe_plan(route, n_experts, tm):
    n = route.shape[0]
    e_flat = route[:, :TOP_K].astype(jnp.int32).reshape(-1)
    onehot = (e_flat[:, None] == jnp.arange(n_experts, dtype=jnp.int32)[None, :]).astype(jnp.int32)
    csum = jnp.cumsum(onehot, axis=0)
    rank = jnp.sum(onehot * csum, axis=1) - 1
    counts = csum[-1]
    padded = ((counts + tm - 1) // tm) * tm
    ends = jnp.cumsum(padded)
    starts = ends - padded
    pos = jnp.sum(onehot * starts[None, :], axis=1) + rank
    p_rows = TOP_K * n + n_experts * tm
    src = jnp.zeros((p_rows,), jnp.int32).at[pos].set(jnp.arange(TOP_K * n, dtype=jnp.int32) // TOP_K)
    tile_start = jnp.arange(p_rows // tm, dtype=jnp.int32) * tm
    tile_e = jnp.minimum(jnp.sum((tile_start[:, None] >= ends[None, :]).astype(jnp.int32), axis=1),
                         n_experts - 1)
    tile_valid = (tile_start < ends[-1]).astype(jnp.int32)
    return pos.reshape(n, TOP_K), src, tile_e, tile_valid


def kernel(x_prompt, x_sample, cache_k, cache_v, c, c_ctx, mod_w, mod_b, norm_g, wq, wk, wv, wo, q_norm, k_norm, conv_pw1, conv_pw1_b, conv_dw, conv_dw_b, conv_ln_g, conv_ln_b, conv_pw2, conv_pw2_b, fnet_w, fnet_b, ffn_w1, ffn_w3, ffn_w2, router_w, moe_w1, moe_w3, moe_w2):
    batch, seq, d = x_prompt.shape
    dec_batch, dec_seq, _ = x_sample.shape
    depth = mod_w.shape[0]
    n_attn, past, n_kv, head_dim = cache_k.shape[1:]
    n_experts = router_w.shape[2]
    st = _Stream(batch, seq, dec_batch, dec_seq)
    tm = st.tm
    assert st.n_ctx % dec_seq == 0 and dec_batch + 1 <= 8 and head_dim == LANES

    y = jnp.concatenate([x_prompt.reshape(st.n_ctx, d), x_sample.reshape(st.n_lat, d)], axis=0)
    cvec8 = jnp.concatenate([c_ctx[None, :], c, jnp.zeros((8 - 1 - dec_batch, d), F32)], axis=0)
    mods = _adaln_table(cvec8, mod_w, mod_b)
    norm_g4 = norm_g.reshape(depth, 2, 1, d)
    cos, sin = _rope_tables(dec_seq, head_dim)
    ffn_w2_b = ffn_w2.astype(BF16)

    def vec3(a):
        return a.reshape(a.shape[0], 1, a.shape[1])

    k_list, v_list = [], []
    for layer in range(depth):
        kind = layer % N_MIXERS
        j = layer // N_MIXERS
        h = _norm_mod(st, y, norm_g4, mods, layer, 0, BF16)
        if kind == 0:
            q, = _mm_headnorm(st, h, wq, j, vec3(q_norm), cos, sin, tm=tm, tn=_tile(d, 512, LANES),
                              want_f32=False)
            kb, kf = _mm_headnorm(st, h, wk, j, vec3(k_norm), cos, sin, tm=tm,
                                  tn=_tile(n_kv * head_dim, 512, LANES), want_f32=True)
            vb, vf = _mm_plain(h, wv, j, tm=tm, tn=_tile(n_kv * head_dim, 512, LANES),
                               out_dtypes=[BF16, F32])
            k_list.append(kf[:st.n_ctx].reshape(batch, seq, n_kv, head_dim))
            v_list.append(vf[:st.n_ctx].reshape(batch, seq, n_kv, head_dim))
            a = _attention(st, q, kb, vb,
                           cache_k[:, j].reshape(dec_batch, past, n_kv * head_dim),
                           cache_v[:, j].reshape(dec_batch, past, n_kv * head_dim), n_kv, head_dim)
            y = _mm_resid(st, a, wo, j, y, mods, layer, 2, tm=tm, tn=_tile(d, 512, LANES))
        elif kind == 1:
            u = _mm_glu(h, conv_pw1, conv_pw1, j, n_out=d, b_col_off=d, tm=tm, tn=_tile(d, 256, LANES),
                        act="glu", out_dtype=F32, bias3=vec3(conv_pw1_b))
            u = _conv_ln(st, u, conv_dw, vec3(conv_dw_b), vec3(conv_ln_g), vec3(conv_ln_b), j)
            y = _mm_resid(st, u, conv_pw2, j, y, mods, layer, 2, tm=tm, tn=_tile(d, 512, LANES),
                          bias3=vec3(conv_pw2_b))
        else:
            fm = _fourier(st, h)
            y = _mm_resid(st, fm, fnet_w, j, y, mods, layer, 2, tm=tm, tn=_tile(d, 512, LANES),
                          bias3=vec3(fnet_b))

        f = layer // 2
        if layer % 2 == 0:
            h = _norm_mod(st, y, norm_g4, mods, layer, 1, BF16)
            dff = ffn_w1.shape[2]
            g = _mm_glu(h, ffn_w1, ffn_w3, f, n_out=dff, b_col_off=0, tm=tm, tn=_tile(dff, 256, LANES),
                        act="swiglu", out_dtype=BF16)
            y = _mm_resid(st, g, ffn_w2_b, f, y, mods, layer, 5, tm=_tile(tm, 512, 16),
                          tn=_tile(d, 512, LANES))
        else:
            rw = jnp.pad(router_w[f], ((0, 0), (0, ROUTE_LANES - n_experts)))
            h, route = _norm_mod(st, y, norm_g4, mods, layer, 1, F32, router_w_pad=rw,
                                 n_experts=n_experts)
            tme = _tile(tm, 512, 16)
            pos, src, tile_e, tile_valid = _moe_plan(route, n_experts, tme)
            xs = _gather_rows(h, src, _tile(tme, 256, 16))
            ye = _moe_experts(xs, tile_e, tile_valid, moe_w1, moe_w3, moe_w2, f, tme)
            y = _moe_combine(st, y, ye, pos, route, mods, layer, 5)

    k_ctx = jnp.stack(k_list, axis=1)
    v_ctx = jnp.stack(v_list, axis=1)
    yp = y[:st.n_ctx].reshape(batch, seq, d)
    ys = y[st.n_ctx:].reshape(dec_batch, dec_seq, d)
    return (yp, ys, k_ctx, v_ctx)
```

```python
import functools
import math

import jax
import jax.numpy as jnp
from jax import lax
from jax.experimental import pallas as pl
from jax.experimental.pallas import tpu as pltpu

F32 = jnp.float32
BF16 = jnp.bfloat16

EPS = 1e-6
ROPE_THETA = 10000.0
GRID_W = 64
N_MIXERS = 3
N_FOURIER_GROUPS = 8
TOP_K = 2
LANES = 128
ROUTE_LANES = 128
VMEM_LIMIT_MB = 60


def _tile(n, pref, mult):
    t = min(pref, n)
    t -= t % mult
    while t >= mult:
        if n % t == 0:
            return t
        t -= mult
    return n


def _params(n_axes, vmem_mb=VMEM_LIMIT_MB):
    return pltpu.CompilerParams(dimension_semantics=("arbitrary",) * n_axes,
                                vmem_limit_bytes=vmem_mb * 1024 * 1024)


def _bdot(a, b):
    return jnp.dot(a.astype(BF16), b.astype(BF16), preferred_element_type=F32)


class _Stream:
    def __init__(self, batch, seq, dec_batch, dec_seq):
        self.batch, self.seq, self.dec_batch, self.dec_seq = batch, seq, dec_batch, dec_seq
        self.n_ctx = batch * seq
        self.n_lat = dec_batch * dec_seq
        self.n_tok = self.n_ctx + self.n_lat
        self.tm = _tile(math.gcd(self.n_ctx, dec_seq), 1024, 16)

    def mod_row(self, i, tm):
        r0 = i * tm
        return jnp.where(r0 < self.n_ctx, 0, 1 + (r0 - self.n_ctx) // self.dec_seq)

    def lat_tile(self, i, tm):
        t0 = self.n_ctx // tm
        return jnp.where(i >= t0, (i - t0) % (self.dec_seq // tm), 0)


def _mod_spec(st, layer, which, tm, width, col_axis):
    if col_axis:
        return pl.BlockSpec((None, None, None, 1, width),
                            lambda i, j: (layer, st.mod_row(i, tm), which, 0, j))
    return pl.BlockSpec((None, None, None, 1, width),
                        lambda i: (layer, st.mod_row(i, tm), which, 0, 0))


def _adaln_kernel(c_ref, w_ref, b_ref, o_ref):
    c = c_ref[...]
    s = c * jax.nn.sigmoid(c)
    o_ref[...] = _bdot(s, w_ref[...]) + b_ref[...]


def _adaln_table(cvec8, mod_w, mod_b):
    depth, d, n6 = mod_w.shape
    tn = _tile(n6, 1024, LANES)
    out = pl.pallas_call(
        _adaln_kernel,
        grid=(depth, n6 // tn),
        in_specs=[pl.BlockSpec((8, d), lambda l, j: (0, 0)),
                  pl.BlockSpec((None, d, tn), lambda l, j: (l, 0, j)),
                  pl.BlockSpec((None, 1, tn), lambda l, j: (l, 0, j))],
        out_specs=pl.BlockSpec((None, 8, tn), lambda l, j: (l, 0, j)),
        out_shape=jax.ShapeDtypeStruct((depth, 8, n6), F32),
        compiler_params=_params(2),
        name="adaln_table",
    )(cvec8, mod_w, mod_b.reshape(depth, 1, n6))
    return out.reshape(depth, 8, 6, 1, d)


def _norm_mod_kernel(y_ref, g_ref, sh_ref, sc_ref, o_ref):
    y = y_ref[...]
    h = y * lax.rsqrt(jnp.mean(y * y, axis=-1, keepdims=True) + EPS) * g_ref[...]
    o_ref[...] = (h * (1.0 + sc_ref[...]) + sh_ref[...]).astype(o_ref.dtype)


def _norm_mod_route_kernel(y_ref, g_ref, sh_ref, sc_ref, rw_ref, o_ref, r_ref, *, n_experts):
    y = y_ref[...]
    h = y * lax.rsqrt(jnp.mean(y * y, axis=-1, keepdims=True) + EPS) * g_ref[...]
    h = h * (1.0 + sc_ref[...]) + sh_ref[...]
    o_ref[...] = h
    logits = jnp.dot(h, rw_ref[...], preferred_element_type=F32,
                     precision=lax.Precision.HIGHEST)
    lane = lax.broadcasted_iota(jnp.int32, logits.shape, 1).astype(F32)
    neg = jnp.float32(-jnp.inf)
    lg = jnp.where(lane < n_experts, logits, neg)
    m1 = jnp.max(lg, axis=-1, keepdims=True)
    i1 = jnp.min(jnp.where(lg == m1, lane, float(ROUTE_LANES)), axis=-1, keepdims=True)
    lg2 = jnp.where(lane == i1, neg, lg)
    m2 = jnp.max(lg2, axis=-1, keepdims=True)
    i2 = jnp.min(jnp.where(lg2 == m2, lane, float(ROUTE_LANES)), axis=-1, keepdims=True)
    e = jnp.exp(m2 - m1)
    g1 = 1.0 / (1.0 + e)
    g2 = e / (1.0 + e)
    r_ref[...] = jnp.where(lane == 0, i1, jnp.where(lane == 1, i2,
                           jnp.where(lane == 2, g1, jnp.where(lane == 3, g2, 0.0))))


def _norm_mod(st, y, norm_g4, mods, layer, sub, out_dtype, router_w_pad=None, n_experts=0):
    n, d = y.shape
    tm = _tile(st.tm, 512, 16)
    in_specs = [pl.BlockSpec((tm, d), lambda i: (i, 0)),
                pl.BlockSpec((None, None, 1, d), lambda i: (layer, sub, 0, 0)),
                _mod_spec(st, layer, 3 * sub + 0, tm, d, False),
                _mod_spec(st, layer, 3 * sub + 1, tm, d, False)]
    if router_w_pad is None:
        return pl.pallas_call(
            _norm_mod_kernel, grid=(n // tm,), in_specs=in_specs,
            out_specs=pl.BlockSpec((tm, d), lambda i: (i, 0)),
            out_shape=jax.ShapeDtypeStruct((n, d), out_dtype),
            compiler_params=_params(1), name="norm_mod",
        )(y, norm_g4, mods, mods)
    in_specs.append(pl.BlockSpec((d, ROUTE_LANES), lambda i: (0, 0)))
    return pl.pallas_call(
        functools.partial(_norm_mod_route_kernel, n_experts=n_experts),
        grid=(n // tm,), in_specs=in_specs,
        out_specs=[pl.BlockSpec((tm, d), lambda i: (i, 0)),
                   pl.BlockSpec((tm, ROUTE_LANES), lambda i: (i, 0))],
        out_shape=[jax.ShapeDtypeStruct((n, d), F32),
                   jax.ShapeDtypeStruct((n, ROUTE_LANES), F32)],
        compiler_params=_params(1), name="norm_mod_route",
    )(y, norm_g4, mods, mods, router_w_pad)


def _mm_plain_kernel(x_ref, w_ref, *rest, has_bias):
    acc = _bdot(x_ref[...], w_ref[...])
    outs = rest
    if has_bias:
        acc = acc + rest[0][...]
        outs = rest[1:]
    for o_ref in outs:
        o_ref[...] = acc.astype(o_ref.dtype)


def _mm_plain(x, w3, wi, *, tm, tn, out_dtypes, bias3=None, block_diag=False):
    m = x.shape[0]
    kb, n_w = w3.shape[1], w3.shape[2]
    if block_diag:
        n_out, tn = x.shape[1], n_w
        x_spec = pl.BlockSpec((tm, kb), lambda i, j: (i, j))
        w_spec = pl.BlockSpec((None, kb, n_w), lambda i, j: (wi, 0, 0))
    else:
        n_out = n_w
        x_spec = pl.BlockSpec((tm, kb), lambda i, j: (i, 0))
        w_spec = pl.BlockSpec((None, kb, tn), lambda i, j: (wi, 0, j))
    in_specs, args = [x_spec, w_spec], [x, w3]
    if bias3 is not None:
        in_specs.append(pl.BlockSpec((None, 1, tn), lambda i, j: (wi, 0, j)))
        args.append(bias3)
    outs = pl.pallas_call(
        functools.partial(_mm_plain_kernel, has_bias=bias3 is not None),
        grid=(m // tm, n_out // tn), in_specs=in_specs,
        out_specs=[pl.BlockSpec((tm, tn), lambda i, j: (i, j)) for _ in out_dtypes],
        out_shape=[jax.ShapeDtypeStruct((m, n_out), dt) for dt in out_dtypes],
        compiler_params=_params(2), name="mm_plain",
    )(*args)
    return outs


def _mm_glu_kernel(x_ref, wa_ref, wb_ref, *rest, has_bias, act):
    x = x_ref[...]
    a = _bdot(x, wa_ref[...])
    b = _bdot(x, wb_ref[...])
    if has_bias:
        a = a + rest[0][...]
        b = b + rest[1][...]
    o_ref = rest[-1]
    if act == "swiglu":
        o_ref[...] = (a * jax.nn.sigmoid(a) * b).astype(o_ref.dtype)
    else:
        o_ref[...] = (a * jax.nn.sigmoid(b)).astype(o_ref.dtype)


def _mm_glu(x, wa3, wb3, wi, *, n_out, b_col_off, tm, tn, act, out_dtype, bias3=None, x_buffers=2):
    m, k = x.shape
    off = b_col_off // tn
    x_mode = {} if x_buffers == 2 else dict(pipeline_mode=pl.Buffered(x_buffers))
    in_specs = [pl.BlockSpec((tm, k), lambda i, j: (i, 0), **x_mode),
                pl.BlockSpec((None, k, tn), lambda i, j: (wi, 0, j)),
                pl.BlockSpec((None, k, tn), lambda i, j: (wi, 0, j + off))]
    args = [x, wa3, wb3]
    if bias3 is not None:
        in_specs += [pl.BlockSpec((None, 1, tn), lambda i, j: (wi, 0, j)),
                     pl.BlockSpec((None, 1, tn), lambda i, j: (wi, 0, j + off))]
        args += [bias3, bias3]
    return pl.pallas_call(
        functools.partial(_mm_glu_kernel, has_bias=bias3 is not None, act=act),
        grid=(m // tm, n_out // tn), in_specs=in_specs,
        out_specs=pl.BlockSpec((tm, tn), lambda i, j: (i, j)),
        out_shape=jax.ShapeDtypeStruct((m, n_out), out_dtype),
        compiler_params=_params(2), name="mm_glu",
    )(*args)


def _mm_resid_kernel(x_ref, w_ref, *rest, has_bias):
    acc = _bdot(x_ref[...], w_ref[...])
    if has_bias:
        acc = acc + rest[0][...]
        rest = rest[1:]
    y_ref, gate_ref, o_ref = rest
    o_ref[...] = y_ref[...] + gate_ref[...] * acc


def _mm_resid(st, x, w3, wi, y, mods, layer, which_gate, *, tm, tn, bias3=None):
    m, k = x.shape
    n = w3.shape[2]
    in_specs = [pl.BlockSpec((tm, k), lambda i, j: (i, 0)),
                pl.BlockSpec((None, k, tn), lambda i, j: (wi, 0, j))]
    args = [x, w3]
    if bias3 is not None:
        in_specs.append(pl.BlockSpec((None, 1, tn), lambda i, j: (wi, 0, j)))
        args.append(bias3)
    in_specs += [pl.BlockSpec((tm, tn), lambda i, j: (i, j)),
                 _mod_spec(st, layer, which_gate, tm, tn, True)]
    args += [y, mods]
    return pl.pallas_call(
        functools.partial(_mm_resid_kernel, has_bias=bias3 is not None),
        grid=(m // tm, n // tn), in_specs=in_specs,
        out_specs=pl.BlockSpec((tm, tn), lambda i, j: (i, j)),
        out_shape=jax.ShapeDtypeStruct((m, n), F32),
        compiler_params=_params(2), name="mm_resid",
    )(*args)


def _headnorm_step(x_ref, w_ref, g_ref, cos_ref, sin_ref, ob_ref, of_ref, acc_new, acc_old):
    tm, tn = ob_ref.shape
    lane = lax.broadcasted_iota(jnp.int32, (tm, LANES), 1)
    low = (lane % (LANES // 2)) < (LANES // 4)
    g = g_ref[...]
    c = cos_ref[...]
    s = sin_ref[...]
    for hh in range(tn // LANES):
        cols = slice(hh * LANES, (hh + 1) * LANES)
        a = acc_old[:, cols]
        r = a * lax.rsqrt(jnp.mean(a * a, axis=-1, keepdims=True) + EPS) * g
        if of_ref is not None:
            of_ref[:, cols] = r
        partner = jnp.where(low, pltpu.roll(r, LANES - LANES // 4, 1), pltpu.roll(r, LANES // 4, 1))
        ob_ref[:, cols] = (r * c + partner * s).astype(ob_ref.dtype)
    acc_new[...] = _bdot(x_ref[...], w_ref[...])


def _mm_headnorm_kernel(x_ref, w_ref, g_ref, cos_ref, sin_ref, ob_ref, *rest):
    of_ref = rest[0] if len(rest) == 3 else None
    acc0, acc1 = rest[-2:]
    t = pl.program_id(0)

    @pl.when(t == 0)
    def _():
        acc1[...] = jnp.zeros_like(acc1)

    step = functools.partial(_headnorm_step, x_ref, w_ref, g_ref, cos_ref, sin_ref, ob_ref, of_ref)
    pl.when(t % 2 == 0)(functools.partial(step, acc0, acc1))
    pl.when(t % 2 == 1)(functools.partial(step, acc1, acc0))


def _mm_headnorm(st, x, w3, wi, g3, cos, sin, *, tm, tn, want_f32):
    m, k = x.shape
    n = w3.shape[2]
    nj = n // tn
    n_tiles = (m // tm) * nj
    ident = st.dec_seq // tm
    cur = lambda t: jnp.minimum(t, n_tiles - 1)
    prv = lambda t: jnp.maximum(t - 1, 0)

    def tab(t):
        i = prv(t) // nj
        return (jnp.where(i >= st.n_ctx // tm, st.lat_tile(i, tm), ident), 0)

    out_idx = lambda t: (prv(t) // nj, prv(t) % nj)
    out_specs = [pl.BlockSpec((tm, tn), out_idx)]
    out_shape = [jax.ShapeDtypeStruct((m, n), BF16)]
    if want_f32:
        out_specs.append(pl.BlockSpec((tm, tn), out_idx))
        out_shape.append(jax.ShapeDtypeStruct((m, n), F32))
    return pl.pallas_call(
        _mm_headnorm_kernel,
        grid=(n_tiles + 1,),
        in_specs=[pl.BlockSpec((tm, k), lambda t: (cur(t) // nj, 0)),
                  pl.BlockSpec((None, k, tn), lambda t: (wi, 0, cur(t) % nj)),
                  pl.BlockSpec((None, 1, LANES), lambda t: (wi, 0, 0)),
                  pl.BlockSpec((tm, LANES), tab),
                  pl.BlockSpec((tm, LANES), tab)],
        out_specs=out_specs, out_shape=out_shape,
        scratch_shapes=[pltpu.VMEM((tm, tn), F32), pltpu.VMEM((tm, tn), F32)],
        compiler_params=_params(1), name="mm_headnorm",
    )(x, w3, g3, cos, sin)


def _rope_tables(dec_seq, head_dim, tm, gain):
    half = head_dim // 2
    t = jnp.arange(dec_seq)
    rows = (t // GRID_W).astype(F32)
    cols = (t % GRID_W).astype(F32)
    inv = ROPE_THETA ** (-jnp.arange(0, half, 2, dtype=F32) / half)
    ar = rows[:, None] * inv
    ac = cols[:, None] * inv
    cos = jnp.concatenate([jnp.cos(ar), jnp.cos(ar), jnp.cos(ac), jnp.cos(ac)], axis=-1)
    sin = jnp.concatenate([-jnp.sin(ar), jnp.sin(ar), -jnp.sin(ac), jnp.sin(ac)], axis=-1)
    cos = jnp.concatenate([cos, jnp.ones((tm, head_dim), F32)], axis=0)
    sin = jnp.concatenate([sin, jnp.zeros((tm, head_dim), F32)], axis=0)
    return cos * gain, sin * gain


ATTN_KEY_CHUNK = 512


def _flash(q4, kvs):
    dn = (((1,), (1,)), ((), ()))
    m = l = acc = None
    for k, v in kvs:
        s = lax.dot_general(q4, k, dn, preferred_element_type=F32)
        mc = s.max(axis=-1, keepdims=True)
        if m is None:
            m_new = mc
            p = jnp.exp2(s - m_new)
            l = p.sum(axis=-1, keepdims=True)
            acc = jnp.dot(p.astype(BF16), v, preferred_element_type=F32)
        else:
            m_new = jnp.maximum(m, mc)
            alpha = jnp.exp2(m - m_new)
            p = jnp.exp2(s - m_new)
            l = alpha * l + p.sum(axis=-1, keepdims=True)
            acc = alpha * acc + jnp.dot(p.astype(BF16), v, preferred_element_type=F32)
        m = m_new
    return acc / l


def _attn_ctx_kernel(q_ref, k_ref, v_ref, o_ref, *, n_seq, n_kv, group, head_dim):
    @pl.when(pl.program_id(0) >= n_seq)
    def _():
        o_ref[...] = jnp.zeros_like(o_ref)

    pl.when(pl.program_id(0) < n_seq)(
        functools.partial(_attn_ctx_body, q_ref, k_ref, v_ref, o_ref, n_kv, group, head_dim))


def _attn_ctx_body(q_ref, k_ref, v_ref, o_ref, n_kv, group, head_dim):
    tq = q_ref.shape[0]
    for kv in range(n_kv):
        kh = k_ref[:, kv * head_dim:(kv + 1) * head_dim]
        vh = v_ref[:, kv * head_dim:(kv + 1) * head_dim]
        c0 = kv * group * head_dim
        q4 = jnp.concatenate([q_ref[:, c0 + g * head_dim:c0 + (g + 1) * head_dim]
                              for g in range(group)], axis=0)
        o = _flash(q4, [(kh, vh)])
        for g in range(group):
            o_ref[:, c0 + g * head_dim:c0 + (g + 1) * head_dim] = (
                o[g * tq:(g + 1) * tq].astype(o_ref.dtype))


def _attn_lat_kernel(q_ref, k_ref, v_ref, ck_ref, cv_ref, prev_ref, o_ref, *, group, head_dim, chunk):
    del prev_ref
    tq = q_ref.shape[0]
    q4 = jnp.concatenate([q_ref[:, g * head_dim:(g + 1) * head_dim] for g in range(group)], axis=0)
    kvs = [(ck_ref[...].astype(BF16), cv_ref[...].astype(BF16))]
    for c0 in range(0, k_ref.shape[0], chunk):
        kvs.append((k_ref[c0:c0 + chunk, :], v_ref[c0:c0 + chunk, :]))
    o = _flash(q4, kvs)
    for g in range(group):
        o_ref[:, g * head_dim:(g + 1) * head_dim] = o[g * tq:(g + 1) * tq].astype(o_ref.dtype)


def _attention(st, q, k, v, cache_k3, cache_v3, n_kv, head_dim):
    n, d = q.shape
    kvd = n_kv * head_dim
    group = d // kvd
    gw = group * head_dim
    ctx = pl.pallas_call(
        functools.partial(_attn_ctx_kernel, n_seq=st.batch, n_kv=n_kv, group=group, head_dim=head_dim),
        grid=(n // st.seq,),
        in_specs=[pl.BlockSpec((st.seq, d), lambda b: (jnp.minimum(b, st.batch - 1), 0)),
                  pl.BlockSpec((st.seq, kvd), lambda b: (jnp.minimum(b, st.batch - 1), 0)),
                  pl.BlockSpec((st.seq, kvd), lambda b: (jnp.minimum(b, st.batch - 1), 0))],
        out_specs=pl.BlockSpec((st.seq, d), lambda b: (b, 0)),
        out_shape=jax.ShapeDtypeStruct((n, d), BF16),
        compiler_params=_params(1), name="attn_ctx",
    )(q, k, v)
    tq = _tile(st.dec_seq, 256, 16)
    qt = st.dec_seq // tq
    q0 = st.n_ctx // tq
    s0 = st.n_ctx // st.dec_seq
    past = cache_k3.shape[1]
    chunk = _tile(st.dec_seq, ATTN_KEY_CHUNK, LANES)
    return pl.pallas_call(
        functools.partial(_attn_lat_kernel, group=group, head_dim=head_dim, chunk=chunk),
        grid=(st.dec_batch, n_kv, qt),
        in_specs=[pl.BlockSpec((tq, gw), lambda b, h, i: (q0 + b * qt + i, h)),
                  pl.BlockSpec((st.dec_seq, head_dim), lambda b, h, i: (s0 + b, h)),
                  pl.BlockSpec((st.dec_seq, head_dim), lambda b, h, i: (s0 + b, h)),
                  pl.BlockSpec((None, past, head_dim), lambda b, h, i: (b, 0, h)),
                  pl.BlockSpec((None, past, head_dim), lambda b, h, i: (b, 0, h)),
                  pl.BlockSpec(memory_space=pl.ANY)],
        out_specs=pl.BlockSpec((tq, gw), lambda b, h, i: (q0 + b * qt + i, h)),
        out_shape=jax.ShapeDtypeStruct((n, d), BF16),
        input_output_aliases={5: 0},
        compiler_params=_params(3), name="attn_lat",
    )(q, k, v, cache_k3, cache_v3, ctx)


CONV_HALO = 16
F32_SUBLANES = 8
LN_ROWS = 16


def _conv_ln_kernel(cur_ref, prev_ref, next_ref, dw_ref, dwb_ref, g_ref, b_ref, o_ref, xp_ref, cv_ref,
                    sh_ref, *, st, tc, width, cw):
    i = pl.program_id(0)
    r0 = i * tc
    in_ctx = r0 < st.n_ctx
    pos = jnp.where(in_ctx, r0 % st.seq, (r0 - st.n_ctx) % st.dec_seq)
    seq_len = jnp.where(in_ctx, st.seq, st.dec_seq)
    first = pos == 0
    last = pos + tc == seq_len
    d = cur_ref.shape[1]
    xp_ref[0:CONV_HALO, :] = jnp.where(first, 0.0, prev_ref[...])
    xp_ref[CONV_HALO:CONV_HALO + tc, :] = cur_ref[...]
    xp_ref[CONV_HALO + tc:2 * CONV_HALO + tc, :] = jnp.where(last, 0.0, next_ref[...])
    base = CONV_HALO - width // 2

    n_a = -(-width // F32_SUBLANES)
    span = sh_ref.shape[1]

    def chunk(c, carry):
        off = pl.multiple_of(c * cw, cw)
        acc = jnp.zeros((tc, cw), F32)
        for b in range(F32_SUBLANES):
            start = base + b
            aligned = start % F32_SUBLANES == 0
            if not aligned:
                sh_ref[b] = xp_ref[pl.ds(start, span), pl.ds(off, cw)]
            for a in range(n_a):
                j = F32_SUBLANES * a + b
                if j >= width:
                    continue
                if aligned:
                    rows = xp_ref[pl.ds(start + F32_SUBLANES * a, tc), pl.ds(off, cw)]
                else:
                    rows = sh_ref[b, pl.ds(F32_SUBLANES * a, tc), :]
                acc = acc + rows * dw_ref[pl.ds(j, 1), pl.ds(off, cw)]
        cv_ref[:, pl.ds(off, cw)] = acc + dwb_ref[:, pl.ds(off, cw)]
        return carry

    lax.fori_loop(0, d // cw, chunk, 0)

    def ln_rows(rb, carry):
        rows = pl.ds(pl.multiple_of(rb * LN_ROWS, LN_ROWS), LN_ROWS)
        u = cv_ref[rows, :]
        mu = jnp.mean(u, axis=-1, keepdims=True)
        uc = u - mu
        var = jnp.mean(uc * uc, axis=-1, keepdims=True)
        yv = uc * lax.rsqrt(var + EPS) * g_ref[...] + b_ref[...]
        o_ref[rows, :] = (yv * jax.nn.sigmoid(yv)).astype(o_ref.dtype)
        return carry

    lax.fori_loop(0, tc // LN_ROWS, ln_rows, 0, unroll=4)


def _conv_ln(st, u, dw3, dwb3, lng3, lnb3, wi):
    n, d = u.shape
    width = dw3.shape[1]
    tc = _tile(math.gcd(st.seq, st.dec_seq), 256, CONV_HALO)
    hb = tc // CONV_HALO
    n_halo = n // CONV_HALO
    cw = LANES
    span = tc + F32_SUBLANES * (-(-width // F32_SUBLANES) - 1)
    vec = lambda: pl.BlockSpec((None, 1, d), lambda i: (wi, 0, 0))
    return pl.pallas_call(
        functools.partial(_conv_ln_kernel, st=st, tc=tc, width=width, cw=cw),
        grid=(n // tc,),
        in_specs=[pl.BlockSpec((tc, d), lambda i: (i, 0)),
                  pl.BlockSpec((CONV_HALO, d), lambda i: (jnp.maximum(i * hb - 1, 0), 0)),
                  pl.BlockSpec((CONV_HALO, d), lambda i: (jnp.minimum((i + 1) * hb, n_halo - 1), 0)),
                  pl.BlockSpec((None, width, d), lambda i: (wi, 0, 0)),
                  vec(), vec(), vec()],
        out_specs=pl.BlockSpec((tc, d), lambda i: (i, 0)),
        out_shape=jax.ShapeDtypeStruct((n, d), BF16),
        scratch_shapes=[pltpu.VMEM((tc + 2 * CONV_HALO, d), F32), pltpu.VMEM((tc, d), F32),
                        pltpu.VMEM((F32_SUBLANES, span, cw), F32)],
        compiler_params=_params(1), name="conv_ln",
    )(u, u, u, dw3, dwb3, lng3, lnb3)


def _dft_mats(n):
    j = jnp.arange(n, dtype=jnp.int32)
    ang = ((j[:, None] * j[None, :]) % n).astype(F32) * (2.0 * math.pi / n)
    return jnp.cos(ang).astype(BF16), jnp.sin(ang).astype(BF16)


def _dft_time_kernel(ct_ref, st_ref, xc_ref, xs_ref, *rest, scale, n_seq):
    o_ref = rest[-1]

    @pl.when(pl.program_id(1) < n_seq)
    def _():
        acc = (jnp.dot(ct_ref[...], xc_ref[...], preferred_element_type=F32)
               - jnp.dot(st_ref[...], xs_ref[...], preferred_element_type=F32))
        o_ref[...] = (acc * scale).astype(o_ref.dtype)

    @pl.when(pl.program_id(1) >= n_seq)
    def _():
        o_ref[...] = jnp.zeros_like(o_ref)


def _dft_time(xc, xs, n_seq, t, row0, scale, prev=None):
    n, d = xc.shape
    ct, sn = _dft_mats(t)
    tr = _tile(t, 512, 16)
    tn = _tile(d, 512 if t > 1024 else 2048, LANES)
    s0 = row0 // t
    r0 = row0 // tr
    rt = t // tr
    n_steps = n_seq if prev is not None else (n - row0) // t
    in_specs = [pl.BlockSpec((tr, t), lambda i, b, j: (i, 0)),
                pl.BlockSpec((tr, t), lambda i, b, j: (i, 0)),
                pl.BlockSpec((t, tn), lambda i, b, j: (s0 + jnp.minimum(b, n_seq - 1), j)),
                pl.BlockSpec((t, tn), lambda i, b, j: (s0 + jnp.minimum(b, n_seq - 1), j))]
    args = [ct, sn, xc, xs]
    aliases = {}
    if prev is not None:
        in_specs.append(pl.BlockSpec(memory_space=pl.ANY))
        args.append(prev)
        aliases = {4: 0}
    return pl.pallas_call(
        functools.partial(_dft_time_kernel, scale=scale, n_seq=n_seq),
        grid=(rt, n_steps, d // tn), in_specs=in_specs,
        out_specs=pl.BlockSpec((tr, tn), lambda i, b, j: (r0 + b * rt + i, j)),
        out_shape=jax.ShapeDtypeStruct((n, d), BF16),
        input_output_aliases=aliases,
        compiler_params=_params(3), name="dft_time",
    )(*args)


def _fourier(st, h):
    n, d = h.shape
    gsz = d // N_FOURIER_GROUPS
    cc, sc = _dft_mats(gsz)
    tm = _tile(n, 2048, 16)
    xc, = _mm_plain(h, cc[None], 0, tm=tm, tn=gsz, out_dtypes=[BF16], block_diag=True)
    xs, = _mm_plain(h, sc[None], 0, tm=tm, tn=gsz, out_dtypes=[BF16], block_diag=True)
    f = _dft_time(xc, xs, st.batch, st.seq, 0, (st.seq * gsz) ** -0.5)
    return _dft_time(xc, xs, st.dec_batch, st.dec_seq, st.n_ctx, (st.dec_seq * gsz) ** -0.5, prev=f)


DMA_ISSUE_UNROLL = 8


def _gather_rows_kernel(src_ref, nxt_ref, h_ref, o_ref, buf_ref, sem):
    tg = buf_ref.shape[1]
    i = pl.program_id(0)
    slot = i % 2

    def issue_rows(idx_ref, dst_slot):
        def body(r, carry):
            pltpu.make_async_copy(h_ref.at[pl.ds(idx_ref[0, r], 1)], buf_ref.at[dst_slot, pl.ds(r, 1)],
                                  sem.at[dst_slot]).start()
            return carry
        lax.fori_loop(0, tg, body, 0, unroll=DMA_ISSUE_UNROLL)

    @pl.when(i == 0)
    def _():
        issue_rows(src_ref, 0)

    @pl.when(i + 1 < pl.num_programs(0))
    def _():
        issue_rows(nxt_ref, 1 - slot)

    pltpu.make_async_copy(h_ref.at[pl.ds(0, tg)], buf_ref.at[slot], sem.at[slot]).wait()
    o_ref[...] = buf_ref[slot].astype(o_ref.dtype)


def _gather_rows(h, src, tg):
    n, d = h.shape
    p = src.shape[0]
    steps = p // tg
    src3 = src.reshape(steps, 1, tg)
    return pl.pallas_call(
        _gather_rows_kernel,
        grid=(steps,),
        in_specs=[pl.BlockSpec((None, 1, tg), lambda i: (i, 0, 0), memory_space=pltpu.SMEM),
                  pl.BlockSpec((None, 1, tg), lambda i: (jnp.minimum(i + 1, steps - 1), 0, 0),
                               memory_space=pltpu.SMEM),
                  pl.BlockSpec(memory_space=pl.ANY)],
        out_specs=pl.BlockSpec((tg, d), lambda i: (i, 0)),
        out_shape=jax.ShapeDtypeStruct((p, d), BF16),
        scratch_shapes=[pltpu.VMEM((2, tg, d), F32), pltpu.SemaphoreType.DMA((2,))],
        compiler_params=_params(1), name="moe_gather",
    )(src3, src3, h)


def _moe_up_kernel(te_ref, tv_ref, x_ref, w1_ref, w3_ref, o_ref):
    i = pl.program_id(1)

    @pl.when(tv_ref[i] == 1)
    def _():
        x = x_ref[...]
        a = _bdot(x, w1_ref[...])
        b = _bdot(x, w3_ref[...])
        o_ref[...] = (a * jax.nn.sigmoid(a) * b).astype(o_ref.dtype)

    @pl.when(tv_ref[i] == 0)
    def _():
        o_ref[...] = jnp.zeros_like(o_ref)


def _moe_down_kernel(te_ref, tv_ref, x_ref, w_ref, o_ref):
    i = pl.program_id(1)

    @pl.when(tv_ref[i] == 1)
    def _():
        o_ref[...] = _bdot(x_ref[...], w_ref[...])

    @pl.when(tv_ref[i] == 0)
    def _():
        o_ref[...] = jnp.zeros_like(o_ref)


def _moe_experts(xs, tile_e, tile_valid, w1, w3, w2, f, tm):
    p, d = xs.shape
    ff = w1.shape[3]
    tn = _tile(ff, 512, LANES)
    g = pl.pallas_call(
        _moe_up_kernel,
        grid_spec=pltpu.PrefetchScalarGridSpec(
            num_scalar_prefetch=2, grid=(ff // tn, p // tm),
            in_specs=[pl.BlockSpec((tm, d), lambda j, i, te, tv: (i, 0)),
                      pl.BlockSpec((None, None, d, tn), lambda j, i, te, tv: (f, te[i], 0, j)),
                      pl.BlockSpec((None, None, d, tn), lambda j, i, te, tv: (f, te[i], 0, j))],
            out_specs=pl.BlockSpec((tm, tn), lambda j, i, te, tv: (i, j))),
        out_shape=jax.ShapeDtypeStruct((p, ff), BF16),
        compiler_params=_params(2), name="moe_up",
    )(tile_e, tile_valid, xs, w1, w3)
    tn2 = _tile(d, 1024, LANES)
    return pl.pallas_call(
        _moe_down_kernel,
        grid_spec=pltpu.PrefetchScalarGridSpec(
            num_scalar_prefetch=2, grid=(d // tn2, p // tm),
            in_specs=[pl.BlockSpec((tm, ff), lambda j, i, te, tv: (i, 0)),
                      pl.BlockSpec((None, None, ff, tn2), lambda j, i, te, tv: (f, te[i], 0, j))],
            out_specs=pl.BlockSpec((tm, tn2), lambda j, i, te, tv: (i, j))),
        out_shape=jax.ShapeDtypeStruct((p, d), F32),
        compiler_params=_params(2), name="moe_down",
    )(tile_e, tile_valid, g, w2)


def _moe_combine_kernel(pos_ref, nxt_ref, route_ref, y_ref, gate_ref, ye_ref, *rest, n_ctx_tiles):
    outs, (buf_ref, sem) = rest[:-2], rest[-2:]
    tc = buf_ref.shape[2]
    i = pl.program_id(0)
    slot = i % 2

    def issue_rows(idx_ref, dst_slot):
        def body(r, carry):
            for k in range(TOP_K):
                pltpu.make_async_copy(ye_ref.at[pl.ds(idx_ref[0, TOP_K * r + k], 1)],
                                      buf_ref.at[dst_slot, k, pl.ds(r, 1)], sem.at[dst_slot]).start()
            return carry
        lax.fori_loop(0, tc, body, 0, unroll=DMA_ISSUE_UNROLL)

    @pl.when(i == 0)
    def _():
        issue_rows(pos_ref, 0)

    @pl.when(i + 1 < pl.num_programs(0))
    def _():
        issue_rows(nxt_ref, 1 - slot)

    for k in range(TOP_K):
        pltpu.make_async_copy(ye_ref.at[pl.ds(0, tc)], buf_ref.at[slot, k], sem.at[slot]).wait()
    route = route_ref[...]
    f = route[:, 2:3] * buf_ref[slot, 0] + route[:, 3:4] * buf_ref[slot, 1]
    res = y_ref[...] + gate_ref[...] * f
    if len(outs) == 1:
        outs[0][...] = res
    else:
        @pl.when(i < n_ctx_tiles)
        def _():
            outs[0][...] = res

        @pl.when(i >= n_ctx_tiles)
        def _():
            outs[1][...] = res


def _moe_combine(st, y, ye, pos, route, mods, layer, which_gate, split):
    n, d = y.shape
    tc = _tile(st.tm, 128, 8)
    steps = n // tc
    n_ctx_tiles = st.n_ctx // tc
    pos3 = pos.reshape(steps, 1, TOP_K * tc)
    if split:
        out_specs = [pl.BlockSpec((tc, d), lambda i: (jnp.minimum(i, n_ctx_tiles - 1), 0)),
                     pl.BlockSpec((tc, d), lambda i: (jnp.maximum(i - n_ctx_tiles, 0), 0))]
        out_shape = [jax.ShapeDtypeStruct((st.n_ctx, d), F32), jax.ShapeDtypeStruct((st.n_lat, d), F32)]
    else:
        out_specs = [pl.BlockSpec((tc, d), lambda i: (i, 0))]
        out_shape = [jax.ShapeDtypeStruct((n, d), F32)]
    return pl.pallas_call(
        functools.partial(_moe_combine_kernel, n_ctx_tiles=n_ctx_tiles),
        grid=(steps,),
        in_specs=[pl.BlockSpec((None, 1, TOP_K * tc), lambda i: (i, 0, 0), memory_space=pltpu.SMEM),
                  pl.BlockSpec((None, 1, TOP_K * tc), lambda i: (jnp.minimum(i + 1, steps - 1), 0, 0),
                               memory_space=pltpu.SMEM),
                  pl.BlockSpec((tc, ROUTE_LANES), lambda i: (i, 0)),
                  pl.BlockSpec((tc, d), lambda i: (i, 0)),
                  _mod_spec(st, layer, which_gate, tc, d, False),
                  pl.BlockSpec(memory_space=pl.ANY)],
        out_specs=out_specs, out_shape=out_shape,
        scratch_shapes=[pltpu.VMEM((2, TOP_K, tc, d), F32), pltpu.SemaphoreType.DMA((2,))],
        compiler_params=_params(1), name="moe_combine",
    )(pos3, pos3, route, y, mods, ye)


def _moe_plan(route, n_experts, tm):
    n = route.shape[0]
    e_flat = route[:, :TOP_K].astype(jnp.int32).reshape(-1)
    onehot = (e_flat[:, None] == jnp.arange(n_experts, dtype=jnp.int32)[None, :]).astype(jnp.int32)
    csum = jnp.cumsum(onehot, axis=0)
    rank = jnp.sum(onehot * csum, axis=1) - 1
    counts = csum[-1]
    padded = ((counts + tm - 1) // tm) * tm
    ends = jnp.cumsum(padded)
    starts = ends - padded
    pos = jnp.sum(onehot * starts[None, :], axis=1) + rank
    p_rows = TOP_K * n + n_experts * tm
    src = jnp.zeros((p_rows,), jnp.int32).at[pos].set(jnp.arange(TOP_K * n, dtype=jnp.int32) // TOP_K)
    tile_start = jnp.arange(p_rows // tm, dtype=jnp.int32) * tm
    tile_e = jnp.minimum(jnp.sum((tile_start[:, None] >= ends[None, :]).astype(jnp.int32), axis=1),
                         n_experts - 1)
    tile_valid = (tile_start < ends[-1]).astype(jnp.int32)
    return pos.reshape(n, TOP_K), src, tile_e, tile_valid


def kernel(x_prompt, x_sample, cache_k, cache_v, c, c_ctx, mod_w, mod_b, norm_g, wq, wk, wv, wo, q_norm, k_norm, conv_pw1, conv_pw1_b, conv_dw, conv_dw_b, conv_ln_g, conv_ln_b, conv_pw2, conv_pw2_b, fnet_w, fnet_b, ffn_w1, ffn_w3, ffn_w2, router_w, moe_w1, moe_w3, moe_w2):
    batch, seq, d = x_prompt.shape
    dec_batch, dec_seq, _ = x_sample.shape
    depth = mod_w.shape[0]
    n_attn, past, n_kv, head_dim = cache_k.shape[1:]
    n_experts = router_w.shape[2]
    st = _Stream(batch, seq, dec_batch, dec_seq)
    tm = st.tm
    assert st.n_ctx % dec_seq == 0 and dec_batch + 1 <= 8 and head_dim == LANES

    y = jnp.concatenate([x_prompt.reshape(st.n_ctx, d), x_sample.reshape(st.n_lat, d)], axis=0)
    cvec8 = jnp.concatenate([c_ctx[None, :], c, jnp.zeros((8 - 1 - dec_batch, d), F32)], axis=0)
    mods = _adaln_table(cvec8, mod_w, mod_b)
    norm_g4 = norm_g.reshape(depth, 2, 1, d)
    q_gain = head_dim ** -0.5 * math.log2(math.e)
    cos_q, sin_q = _rope_tables(dec_seq, head_dim, tm, q_gain)
    cos_k, sin_k = _rope_tables(dec_seq, head_dim, tm, 1.0)
    ffn_w2_b = ffn_w2.astype(BF16)

    def vec3(a):
        return a.reshape(a.shape[0], 1, a.shape[1])

    k_list, v_list = [], []
    for layer in range(depth):
        kind = layer % N_MIXERS
        j = layer // N_MIXERS
        h = _norm_mod(st, y, norm_g4, mods, layer, 0, BF16)
        if kind == 0:
            q, = _mm_headnorm(st, h, wq, j, vec3(q_norm), cos_q, sin_q, tm=tm, tn=_tile(d, 512, LANES),
                              want_f32=False)
            kb, kf = _mm_headnorm(st, h, wk, j, vec3(k_norm), cos_k, sin_k, tm=tm,
                                  tn=_tile(n_kv * head_dim, 512, LANES), want_f32=True)
            vb, vf = _mm_plain(h, wv, j, tm=tm, tn=_tile(n_kv * head_dim, 512, LANES),
                               out_dtypes=[BF16, F32])
            k_list.append(kf[:st.n_ctx].reshape(batch, seq, n_kv, head_dim))
            v_list.append(vf[:st.n_ctx].reshape(batch, seq, n_kv, head_dim))
            a = _attention(st, q, kb, vb,
                           cache_k[:, j].reshape(dec_batch, past, n_kv * head_dim),
                           cache_v[:, j].reshape(dec_batch, past, n_kv * head_dim), n_kv, head_dim)
            y = _mm_resid(st, a, wo, j, y, mods, layer, 2, tm=tm, tn=_tile(d, 512, LANES))
        elif kind == 1:
            u = _mm_glu(h, conv_pw1, conv_pw1, j, n_out=d, b_col_off=d, tm=tm, tn=_tile(d, 256, LANES),
                        act="glu", out_dtype=F32, bias3=vec3(conv_pw1_b))
            u = _conv_ln(st, u, conv_dw, vec3(conv_dw_b), vec3(conv_ln_g), vec3(conv_ln_b), j)
            y = _mm_resid(st, u, conv_pw2, j, y, mods, layer, 2, tm=tm, tn=_tile(d, 512, LANES),
                          bias3=vec3(conv_pw2_b))
        else:
            fm = _fourier(st, h)
            y = _mm_resid(st, fm, fnet_w, j, y, mods, layer, 2, tm=tm, tn=_tile(d, 512, LANES),
                          bias3=vec3(fnet_b))

        f = layer // 2
        if layer % 2 == 0:
            h = _norm_mod(st, y, norm_g4, mods, layer, 1, BF16)
            dff = ffn_w1.shape[2]
            g = _mm_glu(h, ffn_w1, ffn_w3, f, n_out=dff, b_col_off=0, tm=_tile(st.n_tok, 2 * tm, 16),
                        tn=_tile(dff, 256, LANES), act="swiglu", out_dtype=BF16, x_buffers=1)
            y = _mm_resid(st, g, ffn_w2_b, f, y, mods, layer, 5, tm=_tile(tm, 512, 16),
                          tn=_tile(d, 512, LANES))
        else:
            rw = jnp.pad(router_w[f], ((0, 0), (0, ROUTE_LANES - n_experts)))
            h, route = _norm_mod(st, y, norm_g4, mods, layer, 1, F32, router_w_pad=rw,
                                 n_experts=n_experts)
            tme = _tile(tm, 512, 16)
            pos, src, tile_e, tile_valid = _moe_plan(route, n_experts, tme)
            xs = _gather_rows(h, src, _tile(tme, 256, 16))
            ye = _moe_experts(xs, tile_e, tile_valid, moe_w1, moe_w3, moe_w2, f, tme)
            outs = _moe_combine(st, y, ye, pos, route, mods, layer, 5, split=layer == depth - 1)
            y = outs[0] if len(outs) == 1 else outs

    k_ctx = jnp.stack(k_list, axis=1)
    v_ctx = jnp.stack(v_list, axis=1)
    yp, ys = y if isinstance(y, (list, tuple)) else (y[:st.n_ctx], y[st.n_ctx:])
    return (yp.reshape(batch, seq, d), ys.reshape(dec_batch, dec_seq, d), k_ctx, v_ctx)
```

```python
import functools
import math

import jax
import jax.numpy as jnp
from jax import lax
from jax.experimental import pallas as pl
from jax.experimental.pallas import tpu as pltpu

F32 = jnp.float32
BF16 = jnp.bfloat16

EPS = 1e-6
ROPE_THETA = 10000.0
GRID_W = 64
N_MIXERS = 3
N_FOURIER_GROUPS = 8
TOP_K = 2
LANES = 128
ROUTE_LANES = 128
VMEM_LIMIT_MB = 60


def _tile(n, pref, mult):
    t = min(pref, n)
    t -= t % mult
    while t >= mult:
        if n % t == 0:
            return t
        t -= mult
    return n


def _params(n_axes, vmem_mb=VMEM_LIMIT_MB):
    return pltpu.CompilerParams(dimension_semantics=("arbitrary",) * n_axes,
                                vmem_limit_bytes=vmem_mb * 1024 * 1024)


def _bdot(a, b):
    return jnp.dot(a.astype(BF16), b.astype(BF16), preferred_element_type=F32)


class _Stream:
    def __init__(self, batch, seq, dec_batch, dec_seq):
        self.batch, self.seq, self.dec_batch, self.dec_seq = batch, seq, dec_batch, dec_seq
        self.n_ctx = batch * seq
        self.n_lat = dec_batch * dec_seq
        self.n_tok = self.n_ctx + self.n_lat
        self.tm = _tile(math.gcd(self.n_ctx, dec_seq), 1024, 16)

    def mod_row(self, i, tm):
        r0 = i * tm
        return jnp.where(r0 < self.n_ctx, 0, 1 + (r0 - self.n_ctx) // self.dec_seq)

    def lat_tile(self, i, tm):
        t0 = self.n_ctx // tm
        return jnp.where(i >= t0, (i - t0) % (self.dec_seq // tm), 0)


def _mod_spec(st, layer, which, tm, width, col_axis):
    if col_axis:
        return pl.BlockSpec((None, None, None, 1, width),
                            lambda i, j: (layer, st.mod_row(i, tm), which, 0, j))
    return pl.BlockSpec((None, None, None, 1, width),
                        lambda i: (layer, st.mod_row(i, tm), which, 0, 0))


def _resid_specs(st, y, tm, tn, two_axes):
    if not isinstance(y, (tuple, list)):
        return [pl.BlockSpec((tm, tn), (lambda i, j: (i, j)) if two_axes else (lambda i: (i, 0)))], [y]
    t0 = st.n_ctx // tm
    if two_axes:
        maps = [lambda i, j: (jnp.minimum(i, t0 - 1), j), lambda i, j: (jnp.maximum(i - t0, 0), j)]
    else:
        maps = [lambda i: (jnp.minimum(i, t0 - 1), 0), lambda i: (jnp.maximum(i - t0, 0), 0)]
    return [pl.BlockSpec((tm, tn), m) for m in maps], list(y)


def _resid_tile(y_refs, ctx_tiles):
    if len(y_refs) == 1:
        return y_refs[0][...]
    return jnp.where(pl.program_id(0) < ctx_tiles, y_refs[0][...], y_refs[1][...])


def _adaln_kernel(c_ref, w_ref, b_ref, o_ref):
    c = c_ref[...]
    s = c * jax.nn.sigmoid(c)
    o_ref[...] = _bdot(s, w_ref[...]) + b_ref[...]


def _adaln_table(cvec8, mod_w, mod_b):
    depth, d, n6 = mod_w.shape
    tn = _tile(n6, 1024, LANES)
    out = pl.pallas_call(
        _adaln_kernel,
        grid=(depth, n6 // tn),
        in_specs=[pl.BlockSpec((8, d), lambda l, j: (0, 0)),
                  pl.BlockSpec((None, d, tn), lambda l, j: (l, 0, j)),
                  pl.BlockSpec((None, 1, tn), lambda l, j: (l, 0, j))],
        out_specs=pl.BlockSpec((None, 8, tn), lambda l, j: (l, 0, j)),
        out_shape=jax.ShapeDtypeStruct((depth, 8, n6), F32),
        compiler_params=_params(2),
        name="adaln_table",
    )(cvec8, mod_w, mod_b.reshape(depth, 1, n6))
    return out.reshape(depth, 8, 6, 1, d)


def _norm_mod_kernel(*refs, ctx_tiles):
    g_ref, sh_ref, sc_ref, o_ref = refs[-4:]
    y = _resid_tile(refs[:-4], ctx_tiles)
    h = y * lax.rsqrt(jnp.mean(y * y, axis=-1, keepdims=True) + EPS) * g_ref[...]
    o_ref[...] = (h * (1.0 + sc_ref[...]) + sh_ref[...]).astype(o_ref.dtype)


def _norm_mod_route_kernel(y_ref, g_ref, sh_ref, sc_ref, rw_ref, o_ref, r_ref, *, n_experts):
    y = y_ref[...]
    h = y * lax.rsqrt(jnp.mean(y * y, axis=-1, keepdims=True) + EPS) * g_ref[...]
    h = h * (1.0 + sc_ref[...]) + sh_ref[...]
    o_ref[...] = h
    logits = jnp.dot(h, rw_ref[...], preferred_element_type=F32,
                     precision=lax.Precision.HIGHEST)
    lane = lax.broadcasted_iota(jnp.int32, logits.shape, 1).astype(F32)
    neg = jnp.float32(-jnp.inf)
    lg = jnp.where(lane < n_experts, logits, neg)
    m1 = jnp.max(lg, axis=-1, keepdims=True)
    i1 = jnp.min(jnp.where(lg == m1, lane, float(ROUTE_LANES)), axis=-1, keepdims=True)
    lg2 = jnp.where(lane == i1, neg, lg)
    m2 = jnp.max(lg2, axis=-1, keepdims=True)
    i2 = jnp.min(jnp.where(lg2 == m2, lane, float(ROUTE_LANES)), axis=-1, keepdims=True)
    e = jnp.exp(m2 - m1)
    g1 = 1.0 / (1.0 + e)
    g2 = e / (1.0 + e)
    r_ref[...] = jnp.where(lane == 0, i1, jnp.where(lane == 1, i2,
                           jnp.where(lane == 2, g1, jnp.where(lane == 3, g2, 0.0))))


def _norm_mod(st, y, norm_g4, mods, layer, sub, out_dtype, router_w_pad=None, n_experts=0):
    n, d = st.n_tok, norm_g4.shape[-1]
    tm = _tile(st.tm, 512, 16)
    y_specs, y_args = _resid_specs(st, y, tm, d, False)
    in_specs = y_specs + [pl.BlockSpec((None, None, 1, d), lambda i: (layer, sub, 0, 0)),
                          _mod_spec(st, layer, 3 * sub + 0, tm, d, False),
                          _mod_spec(st, layer, 3 * sub + 1, tm, d, False)]
    if router_w_pad is None:
        return pl.pallas_call(
            functools.partial(_norm_mod_kernel, ctx_tiles=st.n_ctx // tm),
            grid=(n // tm,), in_specs=in_specs,
            out_specs=pl.BlockSpec((tm, d), lambda i: (i, 0)),
            out_shape=jax.ShapeDtypeStruct((n, d), out_dtype),
            compiler_params=_params(1), name="norm_mod",
        )(*y_args, norm_g4, mods, mods)
    (y,) = y_args
    in_specs.append(pl.BlockSpec((d, ROUTE_LANES), lambda i: (0, 0)))
    return pl.pallas_call(
        functools.partial(_norm_mod_route_kernel, n_experts=n_experts),
        grid=(n // tm,), in_specs=in_specs,
        out_specs=[pl.BlockSpec((tm, d), lambda i: (i, 0)),
                   pl.BlockSpec((tm, ROUTE_LANES), lambda i: (i, 0))],
        out_shape=[jax.ShapeDtypeStruct((n, d), F32),
                   jax.ShapeDtypeStruct((n, ROUTE_LANES), F32)],
        compiler_params=_params(1), name="norm_mod_route",
    )(y, norm_g4, mods, mods, router_w_pad)


def _mm_plain_kernel(x_ref, w_ref, *rest, has_bias):
    acc = _bdot(x_ref[...], w_ref[...])
    outs = rest
    if has_bias:
        acc = acc + rest[0][...]
        outs = rest[1:]
    for o_ref in outs:
        o_ref[...] = acc.astype(o_ref.dtype)


def _mm_plain(x, w3, wi, *, tm, tn, out_dtypes, bias3=None, block_diag=False):
    m = x.shape[0]
    kb, n_w = w3.shape[1], w3.shape[2]
    if block_diag:
        n_out, tn = x.shape[1], n_w
        x_spec = pl.BlockSpec((tm, kb), lambda i, j: (i, j))
        w_spec = pl.BlockSpec((None, kb, n_w), lambda i, j: (wi, 0, 0))
    else:
        n_out = n_w
        x_spec = pl.BlockSpec((tm, kb), lambda i, j: (i, 0))
        w_spec = pl.BlockSpec((None, kb, tn), lambda i, j: (wi, 0, j))
    in_specs, args = [x_spec, w_spec], [x, w3]
    if bias3 is not None:
        in_specs.append(pl.BlockSpec((None, 1, tn), lambda i, j: (wi, 0, j)))
        args.append(bias3)
    outs = pl.pallas_call(
        functools.partial(_mm_plain_kernel, has_bias=bias3 is not None),
        grid=(m // tm, n_out // tn), in_specs=in_specs,
        out_specs=[pl.BlockSpec((tm, tn), lambda i, j: (i, j)) for _ in out_dtypes],
        out_shape=[jax.ShapeDtypeStruct((m, n_out), dt) for dt in out_dtypes],
        compiler_params=_params(2), name="mm_plain",
    )(*args)
    return outs


def _mm_glu_kernel(x_ref, wa_ref, wb_ref, *rest, has_bias, act):
    x = x_ref[...]
    a = _bdot(x, wa_ref[...])
    b = _bdot(x, wb_ref[...])
    if has_bias:
        a = a + rest[0][...]
        b = b + rest[1][...]
    o_ref = rest[-1]
    if act == "swiglu":
        o_ref[...] = (a * jax.nn.sigmoid(a) * b).astype(o_ref.dtype)
    else:
        o_ref[...] = (a * jax.nn.sigmoid(b)).astype(o_ref.dtype)


def _mm_glu(x, wa3, wb3, wi, *, n_out, b_col_off, tm, tn, act, out_dtype, bias3=None, x_buffers=2):
    m, k = x.shape
    off = b_col_off // tn
    x_mode = {} if x_buffers == 2 else dict(pipeline_mode=pl.Buffered(x_buffers))
    in_specs = [pl.BlockSpec((tm, k), lambda i, j: (i, 0), **x_mode),
                pl.BlockSpec((None, k, tn), lambda i, j: (wi, 0, j)),
                pl.BlockSpec((None, k, tn), lambda i, j: (wi, 0, j + off))]
    args = [x, wa3, wb3]
    if bias3 is not None:
        in_specs += [pl.BlockSpec((None, 1, tn), lambda i, j: (wi, 0, j)),
                     pl.BlockSpec((None, 1, tn), lambda i, j: (wi, 0, j + off))]
        args += [bias3, bias3]
    return pl.pallas_call(
        functools.partial(_mm_glu_kernel, has_bias=bias3 is not None, act=act),
        grid=(m // tm, n_out // tn), in_specs=in_specs,
        out_specs=pl.BlockSpec((tm, tn), lambda i, j: (i, j)),
        out_shape=jax.ShapeDtypeStruct((m, n_out), out_dtype),
        compiler_params=_params(2), name="mm_glu",
    )(*args)


def _mm_resid_kernel(x_ref, w_ref, *rest, has_bias, ctx_tiles):
    acc = _bdot(x_ref[...], w_ref[...])
    if has_bias:
        acc = acc + rest[0][...]
        rest = rest[1:]
    gate_ref, o_ref = rest[-2:]
    o_ref[...] = _resid_tile(rest[:-2], ctx_tiles) + gate_ref[...] * acc


def _mm_resid(st, x, w3, wi, y, mods, layer, which_gate, *, tm, tn, bias3=None):
    m, k = x.shape
    n = w3.shape[2]
    in_specs = [pl.BlockSpec((tm, k), lambda i, j: (i, 0)),
                pl.BlockSpec((None, k, tn), lambda i, j: (wi, 0, j))]
    args = [x, w3]
    if bias3 is not None:
        in_specs.append(pl.BlockSpec((None, 1, tn), lambda i, j: (wi, 0, j)))
        args.append(bias3)
    y_specs, y_args = _resid_specs(st, y, tm, tn, True)
    in_specs += y_specs + [_mod_spec(st, layer, which_gate, tm, tn, True)]
    args += y_args + [mods]
    return pl.pallas_call(
        functools.partial(_mm_resid_kernel, has_bias=bias3 is not None, ctx_tiles=st.n_ctx // tm),
        grid=(m // tm, n // tn), in_specs=in_specs,
        out_specs=pl.BlockSpec((tm, tn), lambda i, j: (i, j)),
        out_shape=jax.ShapeDtypeStruct((m, n), F32),
        compiler_params=_params(2), name="mm_resid",
    )(*args)


def _headnorm_step(x_ref, w_ref, g_ref, cos_ref, sin_ref, ob_ref, of_ref, acc_new, acc_old):
    tm, tn = ob_ref.shape
    lane = lax.broadcasted_iota(jnp.int32, (tm, LANES), 1)
    low = (lane % (LANES // 2)) < (LANES // 4)
    g = g_ref[...]
    c = cos_ref[...]
    s = sin_ref[...]
    for hh in range(tn // LANES):
        cols = slice(hh * LANES, (hh + 1) * LANES)
        a = acc_old[:, cols]
        r = a * lax.rsqrt(jnp.mean(a * a, axis=-1, keepdims=True) + EPS) * g
        if of_ref is not None:
            of_ref[:, cols] = r
        partner = jnp.where(low, pltpu.roll(r, LANES - LANES // 4, 1), pltpu.roll(r, LANES // 4, 1))
        ob_ref[:, cols] = (r * c + partner * s).astype(ob_ref.dtype)
    acc_new[...] = _bdot(x_ref[...], w_ref[...])


def _mm_headnorm_kernel(x_ref, w_ref, g_ref, cos_ref, sin_ref, ob_ref, *rest):
    of_ref = rest[0] if len(rest) == 3 else None
    acc0, acc1 = rest[-2:]
    t = pl.program_id(0)

    @pl.when(t == 0)
    def _():
        acc1[...] = jnp.zeros_like(acc1)

    step = functools.partial(_headnorm_step, x_ref, w_ref, g_ref, cos_ref, sin_ref, ob_ref, of_ref)
    pl.when(t % 2 == 0)(functools.partial(step, acc0, acc1))
    pl.when(t % 2 == 1)(functools.partial(step, acc1, acc0))


def _mm_headnorm(st, x, w3, wi, g3, cos, sin, *, tm, tn, want_f32):
    m, k = x.shape
    n = w3.shape[2]
    nj = n // tn
    n_tiles = (m // tm) * nj
    ident = st.dec_seq // tm
    cur = lambda t: jnp.minimum(t, n_tiles - 1)
    prv = lambda t: jnp.maximum(t - 1, 0)

    def tab(t):
        i = prv(t) // nj
        return (jnp.where(i >= st.n_ctx // tm, st.lat_tile(i, tm), ident), 0)

    out_idx = lambda t: (prv(t) // nj, prv(t) % nj)
    out_specs = [pl.BlockSpec((tm, tn), out_idx)]
    out_shape = [jax.ShapeDtypeStruct((m, n), BF16)]
    if want_f32:
        out_specs.append(pl.BlockSpec((tm, tn), out_idx))
        out_shape.append(jax.ShapeDtypeStruct((m, n), F32))
    return pl.pallas_call(
        _mm_headnorm_kernel,
        grid=(n_tiles + 1,),
        in_specs=[pl.BlockSpec((tm, k), lambda t: (cur(t) // nj, 0)),
                  pl.BlockSpec((None, k, tn), lambda t: (wi, 0, cur(t) % nj)),
                  pl.BlockSpec((None, 1, LANES), lambda t: (wi, 0, 0)),
                  pl.BlockSpec((tm, LANES), tab),
                  pl.BlockSpec((tm, LANES), tab)],
        out_specs=out_specs, out_shape=out_shape,
        scratch_shapes=[pltpu.VMEM((tm, tn), F32), pltpu.VMEM((tm, tn), F32)],
        compiler_params=_params(1), name="mm_headnorm",
    )(x, w3, g3, cos, sin)


def _rope_tables(dec_seq, head_dim, tm, gain):
    half = head_dim // 2
    t = jnp.arange(dec_seq)
    rows = (t // GRID_W).astype(F32)
    cols = (t % GRID_W).astype(F32)
    inv = ROPE_THETA ** (-jnp.arange(0, half, 2, dtype=F32) / half)
    ar = rows[:, None] * inv
    ac = cols[:, None] * inv
    cos = jnp.concatenate([jnp.cos(ar), jnp.cos(ar), jnp.cos(ac), jnp.cos(ac)], axis=-1)
    sin = jnp.concatenate([-jnp.sin(ar), jnp.sin(ar), -jnp.sin(ac), jnp.sin(ac)], axis=-1)
    cos = jnp.concatenate([cos, jnp.ones((tm, head_dim), F32)], axis=0)
    sin = jnp.concatenate([sin, jnp.zeros((tm, head_dim), F32)], axis=0)
    return cos * gain, sin * gain


ATTN_KEY_CHUNK = 1024


def _flash(q4, kvs):
    dn = (((1,), (1,)), ((), ()))
    m = l = acc = None
    for k, v in kvs:
        s = lax.dot_general(q4, k, dn, preferred_element_type=F32)
        mc = s.max(axis=-1, keepdims=True)
        if m is None:
            m_new = mc
            p = jnp.exp2(s - m_new)
            l = p.sum(axis=-1, keepdims=True)
            acc = jnp.dot(p.astype(BF16), v, preferred_element_type=F32)
        else:
            m_new = jnp.maximum(m, mc)
            alpha = jnp.exp2(m - m_new)
            p = jnp.exp2(s - m_new)
            l = alpha * l + p.sum(axis=-1, keepdims=True)
            acc = alpha * acc + jnp.dot(p.astype(BF16), v, preferred_element_type=F32)
        m = m_new
    return acc / l


def _attn_ctx_kernel(q_ref, k_ref, v_ref, o_ref, *, n_seq, n_kv, group, head_dim):
    @pl.when(pl.program_id(0) >= n_seq)
    def _():
        o_ref[...] = jnp.zeros_like(o_ref)

    pl.when(pl.program_id(0) < n_seq)(
        functools.partial(_attn_ctx_body, q_ref, k_ref, v_ref, o_ref, n_kv, group, head_dim))


def _attn_ctx_body(q_ref, k_ref, v_ref, o_ref, n_kv, group, head_dim):
    tq = q_ref.shape[0]
    for kv in range(n_kv):
        kh = k_ref[:, kv * head_dim:(kv + 1) * head_dim]
        vh = v_ref[:, kv * head_dim:(kv + 1) * head_dim]
        c0 = kv * group * head_dim
        q4 = jnp.concatenate([q_ref[:, c0 + g * head_dim:c0 + (g + 1) * head_dim]
                              for g in range(group)], axis=0)
        o = _flash(q4, [(kh, vh)])
        for g in range(group):
            o_ref[:, c0 + g * head_dim:c0 + (g + 1) * head_dim] = (
                o[g * tq:(g + 1) * tq].astype(o_ref.dtype))


def _attn_lat_kernel(q_ref, k_ref, v_ref, ck_ref, cv_ref, prev_ref, o_ref, *, group, head_dim, chunk):
    del prev_ref
    tq = q_ref.shape[0]
    q4 = jnp.concatenate([q_ref[:, g * head_dim:(g + 1) * head_dim] for g in range(group)], axis=0)
    kvs = [(ck_ref[...].astype(BF16), cv_ref[...].astype(BF16))]
    for c0 in range(0, k_ref.shape[0], chunk):
        kvs.append((k_ref[c0:c0 + chunk, :], v_ref[c0:c0 + chunk, :]))
    o = _flash(q4, kvs)
    for g in range(group):
        o_ref[:, g * head_dim:(g + 1) * head_dim] = o[g * tq:(g + 1) * tq].astype(o_ref.dtype)


def _attention(st, q, k, v, cache_k3, cache_v3, n_kv, head_dim):
    n, d = q.shape
    kvd = n_kv * head_dim
    group = d // kvd
    gw = group * head_dim
    ctx = pl.pallas_call(
        functools.partial(_attn_ctx_kernel, n_seq=st.batch, n_kv=n_kv, group=group, head_dim=head_dim),
        grid=(n // st.seq,),
        in_specs=[pl.BlockSpec((st.seq, d), lambda b: (jnp.minimum(b, st.batch - 1), 0)),
                  pl.BlockSpec((st.seq, kvd), lambda b: (jnp.minimum(b, st.batch - 1), 0)),
                  pl.BlockSpec((st.seq, kvd), lambda b: (jnp.minimum(b, st.batch - 1), 0))],
        out_specs=pl.BlockSpec((st.seq, d), lambda b: (b, 0)),
        out_shape=jax.ShapeDtypeStruct((n, d), BF16),
        compiler_params=_params(1), name="attn_ctx",
    )(q, k, v)
    tq = _tile(st.dec_seq, 256, 16)
    qt = st.dec_seq // tq
    q0 = st.n_ctx // tq
    s0 = st.n_ctx // st.dec_seq
    past = cache_k3.shape[1]
    chunk = _tile(st.dec_seq, ATTN_KEY_CHUNK, LANES)
    return pl.pallas_call(
        functools.partial(_attn_lat_kernel, group=group, head_dim=head_dim, chunk=chunk),
        grid=(st.dec_batch, n_kv, qt),
        in_specs=[pl.BlockSpec((tq, gw), lambda b, h, i: (q0 + b * qt + i, h)),
                  pl.BlockSpec((st.dec_seq, head_dim), lambda b, h, i: (s0 + b, h)),
                  pl.BlockSpec((st.dec_seq, head_dim), lambda b, h, i: (s0 + b, h)),
                  pl.BlockSpec((None, past, head_dim), lambda b, h, i: (b, 0, h)),
                  pl.BlockSpec((None, past, head_dim), lambda b, h, i: (b, 0, h)),
                  pl.BlockSpec(memory_space=pl.ANY)],
        out_specs=pl.BlockSpec((tq, gw), lambda b, h, i: (q0 + b * qt + i, h)),
        out_shape=jax.ShapeDtypeStruct((n, d), BF16),
        input_output_aliases={5: 0},
        compiler_params=_params(3), name="attn_lat",
    )(q, k, v, cache_k3, cache_v3, ctx)


CONV_HALO = 16
F32_SUBLANES = 8
LN_ROWS = 16


def _conv_ln_kernel(cur_ref, prev_ref, next_ref, dw_ref, dwb_ref, g_ref, b_ref, o_ref, xp_ref, cv_ref,
                    sh_ref, *, st, tc, width, cw):
    i = pl.program_id(0)
    r0 = i * tc
    in_ctx = r0 < st.n_ctx
    pos = jnp.where(in_ctx, r0 % st.seq, (r0 - st.n_ctx) % st.dec_seq)
    seq_len = jnp.where(in_ctx, st.seq, st.dec_seq)
    first = pos == 0
    last = pos + tc == seq_len
    d = cur_ref.shape[1]
    xp_ref[0:CONV_HALO, :] = jnp.where(first, 0.0, prev_ref[...])
    xp_ref[CONV_HALO:CONV_HALO + tc, :] = cur_ref[...]
    xp_ref[CONV_HALO + tc:2 * CONV_HALO + tc, :] = jnp.where(last, 0.0, next_ref[...])
    base = CONV_HALO - width // 2

    n_a = -(-width // F32_SUBLANES)
    span = sh_ref.shape[1]

    def chunk(c, carry):
        off = pl.multiple_of(c * cw, cw)
        acc = jnp.zeros((tc, cw), F32)
        for b in range(F32_SUBLANES):
            start = base + b
            aligned = start % F32_SUBLANES == 0
            if not aligned:
                sh_ref[b] = xp_ref[pl.ds(start, span), pl.ds(off, cw)]
            for a in range(n_a):
                j = F32_SUBLANES * a + b
                if j >= width:
                    continue
                if aligned:
                    rows = xp_ref[pl.ds(start + F32_SUBLANES * a, tc), pl.ds(off, cw)]
                else:
                    rows = sh_ref[b, pl.ds(F32_SUBLANES * a, tc), :]
                acc = acc + rows * dw_ref[pl.ds(j, 1), pl.ds(off, cw)]
        cv_ref[:, pl.ds(off, cw)] = acc + dwb_ref[:, pl.ds(off, cw)]
        return carry

    lax.fori_loop(0, d // cw, chunk, 0)

    def ln_rows(rb, carry):
        rows = pl.ds(pl.multiple_of(rb * LN_ROWS, LN_ROWS), LN_ROWS)
        u = cv_ref[rows, :]
        mu = jnp.mean(u, axis=-1, keepdims=True)
        uc = u - mu
        var = jnp.mean(uc * uc, axis=-1, keepdims=True)
        yv = uc * lax.rsqrt(var + EPS) * g_ref[...] + b_ref[...]
        o_ref[rows, :] = (yv * jax.nn.sigmoid(yv)).astype(o_ref.dtype)
        return carry

    lax.fori_loop(0, tc // LN_ROWS, ln_rows, 0, unroll=4)


def _conv_ln(st, u, dw3, dwb3, lng3, lnb3, wi):
    n, d = u.shape
    width = dw3.shape[1]
    tc = _tile(math.gcd(st.seq, st.dec_seq), 256, CONV_HALO)
    hb = tc // CONV_HALO
    n_halo = n // CONV_HALO
    cw = LANES
    span = tc + F32_SUBLANES * (-(-width // F32_SUBLANES) - 1)
    vec = lambda: pl.BlockSpec((None, 1, d), lambda i: (wi, 0, 0))
    return pl.pallas_call(
        functools.partial(_conv_ln_kernel, st=st, tc=tc, width=width, cw=cw),
        grid=(n // tc,),
        in_specs=[pl.BlockSpec((tc, d), lambda i: (i, 0)),
                  pl.BlockSpec((CONV_HALO, d), lambda i: (jnp.maximum(i * hb - 1, 0), 0)),
                  pl.BlockSpec((CONV_HALO, d), lambda i: (jnp.minimum((i + 1) * hb, n_halo - 1), 0)),
                  pl.BlockSpec((None, width, d), lambda i: (wi, 0, 0)),
                  vec(), vec(), vec()],
        out_specs=pl.BlockSpec((tc, d), lambda i: (i, 0)),
        out_shape=jax.ShapeDtypeStruct((n, d), BF16),
        scratch_shapes=[pltpu.VMEM((tc + 2 * CONV_HALO, d), F32), pltpu.VMEM((tc, d), F32),
                        pltpu.VMEM((F32_SUBLANES, span, cw), F32)],
        compiler_params=_params(1), name="conv_ln",
    )(u, u, u, dw3, dwb3, lng3, lnb3)


def _dft_mats(n):
    nb = 1 << (n.bit_length() // 2)
    na = n // nb
    k = jnp.arange(n, dtype=jnp.int32)

    def thin(step, rows):
        ang = ((jnp.arange(rows, dtype=jnp.int32)[:, None] * step * k[None, :]) % n).astype(F32)
        ang = ang * (2.0 * math.pi / n)
        return jnp.cos(ang), jnp.sin(ang)

    ca, sa = (t[:, None, :] for t in thin(nb, na))
    cb, sb = (t[None, :, :] for t in thin(1, nb))
    cos = (ca * cb - sa * sb).reshape(n, n)
    sin = (sa * cb + ca * sb).reshape(n, n)
    return cos.astype(BF16), sin.astype(BF16)


def _dft_time_kernel(ct_ref, st_ref, xc_ref, xs_ref, *rest, scale, n_seq):
    o_ref = rest[-1]

    @pl.when(pl.program_id(1) < n_seq)
    def _():
        acc = (jnp.dot(ct_ref[...], xc_ref[...], preferred_element_type=F32)
               - jnp.dot(st_ref[...], xs_ref[...], preferred_element_type=F32))
        o_ref[...] = (acc * scale).astype(o_ref.dtype)

    @pl.when(pl.program_id(1) >= n_seq)
    def _():
        o_ref[...] = jnp.zeros_like(o_ref)


def _dft_time(xc, xs, n_seq, t, row0, scale, prev=None):
    n, d = xc.shape
    ct, sn = _dft_mats(t)
    tr = _tile(t, 512, 16)
    tn = _tile(d, 512 if t > 1024 else 2048, LANES)
    s0 = row0 // t
    r0 = row0 // tr
    rt = t // tr
    n_steps = n_seq if prev is not None else (n - row0) // t
    in_specs = [pl.BlockSpec((tr, t), lambda i, b, j: (i, 0)),
                pl.BlockSpec((tr, t), lambda i, b, j: (i, 0)),
                pl.BlockSpec((t, tn), lambda i, b, j: (s0 + jnp.minimum(b, n_seq - 1), j)),
                pl.BlockSpec((t, tn), lambda i, b, j: (s0 + jnp.minimum(b, n_seq - 1), j))]
    args = [ct, sn, xc, xs]
    aliases = {}
    if prev is not None:
        in_specs.append(pl.BlockSpec(memory_space=pl.ANY))
        args.append(prev)
        aliases = {4: 0}
    return pl.pallas_call(
        functools.partial(_dft_time_kernel, scale=scale, n_seq=n_seq),
        grid=(rt, n_steps, d // tn), in_specs=in_specs,
        out_specs=pl.BlockSpec((tr, tn), lambda i, b, j: (r0 + b * rt + i, j)),
        out_shape=jax.ShapeDtypeStruct((n, d), BF16),
        input_output_aliases=aliases,
        compiler_params=_params(3), name="dft_time",
    )(*args)


def _fourier(st, h):
    n, d = h.shape
    gsz = d // N_FOURIER_GROUPS
    cc, sc = _dft_mats(gsz)
    tm = _tile(n, 2048, 16)
    xc, = _mm_plain(h, cc[None], 0, tm=tm, tn=gsz, out_dtypes=[BF16], block_diag=True)
    xs, = _mm_plain(h, sc[None], 0, tm=tm, tn=gsz, out_dtypes=[BF16], block_diag=True)
    f = _dft_time(xc, xs, st.batch, st.seq, 0, (st.seq * gsz) ** -0.5)
    return _dft_time(xc, xs, st.dec_batch, st.dec_seq, st.n_ctx, (st.dec_seq * gsz) ** -0.5, prev=f)


DMA_ISSUE_UNROLL = 8
DMA_PRIORITIES = 2


def _gather_rows_kernel(src_ref, nxt_ref, h_ref, o_ref, buf_ref, sem):
    tg = buf_ref.shape[1]
    i = pl.program_id(0)
    slot = i % 2

    def issue_rows(idx_ref, dst_slot):
        def body(rp, carry):
            for prio in range(DMA_PRIORITIES):
                r = DMA_PRIORITIES * rp + prio
                pltpu.make_async_copy(h_ref.at[pl.ds(idx_ref[0, r], 1)], buf_ref.at[dst_slot, pl.ds(r, 1)],
                                      sem.at[dst_slot]).start(priority=prio)
            return carry
        lax.fori_loop(0, tg // DMA_PRIORITIES, body, 0, unroll=DMA_ISSUE_UNROLL)

    @pl.when(i == 0)
    def _():
        issue_rows(src_ref, 0)

    @pl.when(i + 1 < pl.num_programs(0))
    def _():
        issue_rows(nxt_ref, 1 - slot)

    pltpu.make_async_copy(h_ref.at[pl.ds(0, tg)], buf_ref.at[slot], sem.at[slot]).wait()
    o_ref[...] = buf_ref[slot].astype(o_ref.dtype)


def _gather_rows(h, src, tg):
    n, d = h.shape
    p = src.shape[0]
    steps = p // tg
    src3 = src.reshape(steps, 1, tg)
    return pl.pallas_call(
        _gather_rows_kernel,
        grid=(steps,),
        in_specs=[pl.BlockSpec((None, 1, tg), lambda i: (i, 0, 0), memory_space=pltpu.SMEM),
                  pl.BlockSpec((None, 1, tg), lambda i: (jnp.minimum(i + 1, steps - 1), 0, 0),
                               memory_space=pltpu.SMEM),
                  pl.BlockSpec(memory_space=pl.ANY)],
        out_specs=pl.BlockSpec((tg, d), lambda i: (i, 0)),
        out_shape=jax.ShapeDtypeStruct((p, d), BF16),
        scratch_shapes=[pltpu.VMEM((2, tg, d), F32), pltpu.SemaphoreType.DMA((2,))],
        compiler_params=_params(1), name="moe_gather",
    )(src3, src3, h)


def _moe_up_kernel(te_ref, tv_ref, x_ref, w1_ref, w3_ref, o_ref):
    i = pl.program_id(1)

    @pl.when(tv_ref[i] == 1)
    def _():
        x = x_ref[...]
        a = _bdot(x, w1_ref[...])
        b = _bdot(x, w3_ref[...])
        o_ref[...] = (a * jax.nn.sigmoid(a) * b).astype(o_ref.dtype)

    @pl.when(tv_ref[i] == 0)
    def _():
        o_ref[...] = jnp.zeros_like(o_ref)


def _moe_down_kernel(te_ref, tv_ref, x_ref, w_ref, o_ref):
    i = pl.program_id(1)

    @pl.when(tv_ref[i] == 1)
    def _():
        o_ref[...] = _bdot(x_ref[...], w_ref[...])

    @pl.when(tv_ref[i] == 0)
    def _():
        o_ref[...] = jnp.zeros_like(o_ref)


def _moe_experts(xs, tile_e, tile_valid, w1, w3, w2, f, tm):
    p, d = xs.shape
    ff = w1.shape[3]
    tn = _tile(ff, 512, LANES)
    g = pl.pallas_call(
        _moe_up_kernel,
        grid_spec=pltpu.PrefetchScalarGridSpec(
            num_scalar_prefetch=2, grid=(ff // tn, p // tm),
            in_specs=[pl.BlockSpec((tm, d), lambda j, i, te, tv: (i, 0)),
                      pl.BlockSpec((None, None, d, tn), lambda j, i, te, tv: (f, te[i], 0, j)),
                      pl.BlockSpec((None, None, d, tn), lambda j, i, te, tv: (f, te[i], 0, j))],
            out_specs=pl.BlockSpec((tm, tn), lambda j, i, te, tv: (i, j))),
        out_shape=jax.ShapeDtypeStruct((p, ff), BF16),
        compiler_params=_params(2), name="moe_up",
    )(tile_e, tile_valid, xs, w1, w3)
    tn2 = _tile(d, 1024, LANES)
    return pl.pallas_call(
        _moe_down_kernel,
        grid_spec=pltpu.PrefetchScalarGridSpec(
            num_scalar_prefetch=2, grid=(d // tn2, p // tm),
            in_specs=[pl.BlockSpec((tm, ff), lambda j, i, te, tv: (i, 0)),
                      pl.BlockSpec((None, None, ff, tn2), lambda j, i, te, tv: (f, te[i], 0, j))],
            out_specs=pl.BlockSpec((tm, tn2), lambda j, i, te, tv: (i, j))),
        out_shape=jax.ShapeDtypeStruct((p, d), F32),
        compiler_params=_params(2), name="moe_down",
    )(tile_e, tile_valid, g, w2)


def _moe_combine_kernel(pos_ref, nxt_ref, route_ref, y_ref, gate_ref, ye_ref, *rest, n_ctx_tiles):
    outs, (buf_ref, sem) = rest[:-2], rest[-2:]
    tc = buf_ref.shape[2]
    i = pl.program_id(0)
    slot = i % 2

    def issue_rows(idx_ref, dst_slot):
        def body(r, carry):
            for k in range(TOP_K):
                pltpu.make_async_copy(ye_ref.at[pl.ds(idx_ref[0, TOP_K * r + k], 1)],
                                      buf_ref.at[dst_slot, k, pl.ds(r, 1)],
                                      sem.at[dst_slot]).start(priority=k % DMA_PRIORITIES)
            return carry
        lax.fori_loop(0, tc, body, 0, unroll=DMA_ISSUE_UNROLL)

    @pl.when(i == 0)
    def _():
        issue_rows(pos_ref, 0)

    @pl.when(i + 1 < pl.num_programs(0))
    def _():
        issue_rows(nxt_ref, 1 - slot)

    for k in range(TOP_K):
        pltpu.make_async_copy(ye_ref.at[pl.ds(0, tc)], buf_ref.at[slot, k], sem.at[slot]).wait()
    route = route_ref[...]
    f = route[:, 2:3] * buf_ref[slot, 0] + route[:, 3:4] * buf_ref[slot, 1]
    res = y_ref[...] + gate_ref[...] * f
    if len(outs) == 1:
        outs[0][...] = res
    else:
        @pl.when(i < n_ctx_tiles)
        def _():
            outs[0][...] = res

        @pl.when(i >= n_ctx_tiles)
        def _():
            outs[1][...] = res


def _moe_combine(st, y, ye, pos, route, mods, layer, which_gate, split):
    n, d = y.shape
    tc = _tile(st.tm, 128, 8)
    steps = n // tc
    n_ctx_tiles = st.n_ctx // tc
    pos3 = pos.reshape(steps, 1, TOP_K * tc)
    if split:
        out_specs = [pl.BlockSpec((tc, d), lambda i: (jnp.minimum(i, n_ctx_tiles - 1), 0)),
                     pl.BlockSpec((tc, d), lambda i: (jnp.maximum(i - n_ctx_tiles, 0), 0))]
        out_shape = [jax.ShapeDtypeStruct((st.n_ctx, d), F32), jax.ShapeDtypeStruct((st.n_lat, d), F32)]
    else:
        out_specs = [pl.BlockSpec((tc, d), lambda i: (i, 0))]
        out_shape = [jax.ShapeDtypeStruct((n, d), F32)]
    return pl.pallas_call(
        functools.partial(_moe_combine_kernel, n_ctx_tiles=n_ctx_tiles),
        grid=(steps,),
        in_specs=[pl.BlockSpec((None, 1, TOP_K * tc), lambda i: (i, 0, 0), memory_space=pltpu.SMEM),
                  pl.BlockSpec((None, 1, TOP_K * tc), lambda i: (jnp.minimum(i + 1, steps - 1), 0, 0),
                               memory_space=pltpu.SMEM),
                  pl.BlockSpec((tc, ROUTE_LANES), lambda i: (i, 0)),
                  pl.BlockSpec((tc, d), lambda i: (i, 0)),
                  _mod_spec(st, layer, which_gate, tc, d, False),
                  pl.BlockSpec(memory_space=pl.ANY)],
        out_specs=out_specs, out_shape=out_shape,
        scratch_shapes=[pltpu.VMEM((2, TOP_K, tc, d), F32), pltpu.SemaphoreType.DMA((2,))],
        compiler_params=_params(1), name="moe_combine",
    )(pos3, pos3, route, y, mods, ye)


def _moe_plan(route, n_experts, tm):
    n = route.shape[0]
    e_flat = route[:, :TOP_K].astype(jnp.int32).reshape(-1)
    onehot = (e_flat[:, None] == jnp.arange(n_experts, dtype=jnp.int32)[None, :]).astype(jnp.int32)
    csum = jnp.cumsum(onehot, axis=0)
    rank = jnp.sum(onehot * csum, axis=1) - 1
    counts = csum[-1]
    padded = ((counts + tm - 1) // tm) * tm
    ends = jnp.cumsum(padded)
    starts = ends - padded
    pos = jnp.sum(onehot * starts[None, :], axis=1) + rank
    p_rows = TOP_K * n + n_experts * tm
    src = jnp.zeros((p_rows,), jnp.int32).at[pos].set(jnp.arange(TOP_K * n, dtype=jnp.int32) // TOP_K)
    tile_start = jnp.arange(p_rows // tm, dtype=jnp.int32) * tm
    tile_e = jnp.minimum(jnp.sum((tile_start[:, None] >= ends[None, :]).astype(jnp.int32), axis=1),
                         n_experts - 1)
    tile_valid = (tile_start < ends[-1]).astype(jnp.int32)
    return pos.reshape(n, TOP_K), src, tile_e, tile_valid


def kernel(x_prompt, x_sample, cache_k, cache_v, c, c_ctx, mod_w, mod_b, norm_g, wq, wk, wv, wo, q_norm, k_norm, conv_pw1, conv_pw1_b, conv_dw, conv_dw_b, conv_ln_g, conv_ln_b, conv_pw2, conv_pw2_b, fnet_w, fnet_b, ffn_w1, ffn_w3, ffn_w2, router_w, moe_w1, moe_w3, moe_w2):
    batch, seq, d = x_prompt.shape
    dec_batch, dec_seq, _ = x_sample.shape
    depth = mod_w.shape[0]
    n_attn, past, n_kv, head_dim = cache_k.shape[1:]
    n_experts = router_w.shape[2]
    st = _Stream(batch, seq, dec_batch, dec_seq)
    tm = st.tm
    assert st.n_ctx % dec_seq == 0 and dec_batch + 1 <= 8 and head_dim == LANES

    y = (x_prompt.reshape(st.n_ctx, d), x_sample.reshape(st.n_lat, d))
    cvec8 = jnp.concatenate([c_ctx[None, :], c, jnp.zeros((8 - 1 - dec_batch, d), F32)], axis=0)
    mods = _adaln_table(cvec8, mod_w, mod_b)
    norm_g4 = norm_g.reshape(depth, 2, 1, d)
    q_gain = head_dim ** -0.5 * math.log2(math.e)
    cos_q, sin_q = _rope_tables(dec_seq, head_dim, tm, q_gain)
    cos_k, sin_k = _rope_tables(dec_seq, head_dim, tm, 1.0)
    ffn_w2_b = ffn_w2.astype(BF16)

    def vec3(a):
        return a.reshape(a.shape[0], 1, a.shape[1])

    k_list, v_list = [], []
    for layer in range(depth):
        kind = layer % N_MIXERS
        j = layer // N_MIXERS
        h = _norm_mod(st, y, norm_g4, mods, layer, 0, BF16)
        if kind == 0:
            q, = _mm_headnorm(st, h, wq, j, vec3(q_norm), cos_q, sin_q, tm=tm, tn=_tile(d, 512, LANES),
                              want_f32=False)
            kb, kf = _mm_headnorm(st, h, wk, j, vec3(k_norm), cos_k, sin_k, tm=tm,
                                  tn=_tile(n_kv * head_dim, 512, LANES), want_f32=True)
            vb, vf = _mm_plain(h, wv, j, tm=tm, tn=_tile(n_kv * head_dim, 512, LANES),
                               out_dtypes=[BF16, F32])
            k_list.append(kf[:st.n_ctx].reshape(batch, seq, n_kv, head_dim))
            v_list.append(vf[:st.n_ctx].reshape(batch, seq, n_kv, head_dim))
            a = _attention(st, q, kb, vb,
                           cache_k[:, j].reshape(dec_batch, past, n_kv * head_dim),
                           cache_v[:, j].reshape(dec_batch, past, n_kv * head_dim), n_kv, head_dim)
            y = _mm_resid(st, a, wo, j, y, mods, layer, 2, tm=tm, tn=_tile(d, 512, LANES))
        elif kind == 1:
            u = _mm_glu(h, conv_pw1, conv_pw1, j, n_out=d, b_col_off=d, tm=tm, tn=_tile(d, 256, LANES),
                        act="glu", out_dtype=F32, bias3=vec3(conv_pw1_b))
            u = _conv_ln(st, u, conv_dw, vec3(conv_dw_b), vec3(conv_ln_g), vec3(conv_ln_b), j)
            y = _mm_resid(st, u, conv_pw2, j, y, mods, layer, 2, tm=tm, tn=_tile(d, 512, LANES),
                          bias3=vec3(conv_pw2_b))
        else:
            fm = _fourier(st, h)
            y = _mm_resid(st, fm, fnet_w, j, y, mods, layer, 2, tm=tm, tn=_tile(d, 512, LANES),
                          bias3=vec3(fnet_b))

        f = layer // 2
        if layer % 2 == 0:
            h = _norm_mod(st, y, norm_g4, mods, layer, 1, BF16)
            dff = ffn_w1.shape[2]
            g = _mm_glu(h, ffn_w1, ffn_w3, f, n_out=dff, b_col_off=0, tm=_tile(st.n_tok, 2 * tm, 16),
                        tn=_tile(dff, 256, LANES), act="swiglu", out_dtype=BF16, x_buffers=1)
            y = _mm_resid(st, g, ffn_w2_b, f, y, mods, layer, 5, tm=_tile(tm, 512, 16),
                          tn=_tile(d, 512, LANES))
        else:
            rw = jnp.pad(router_w[f], ((0, 0), (0, ROUTE_LANES - n_experts)))
            h, route = _norm_mod(st, y, norm_g4, mods, layer, 1, F32, router_w_pad=rw,
                                 n_experts=n_experts)
            tme = _tile(tm, 512, 16)
            pos, src, tile_e, tile_valid = _moe_plan(route, n_experts, tme)
            xs = _gather_rows(h, src, _tile(tme, 256, 16))
            ye = _moe_experts(xs, tile_e, tile_valid, moe_w1, moe_w3, moe_w2, f, tme)
            outs = _moe_combine(st, y, ye, pos, route, mods, layer, 5, split=layer == depth - 1)
            y = outs[0] if len(outs) == 1 else outs

    k_ctx = jnp.stack(k_list, axis=1)
    v_ctx = jnp.stack(v_list, axis=1)
    yp, ys = y if isinstance(y, (list, tuple)) else (y[:st.n_ctx], y[st.n_ctx:])
    return (yp.reshape(batch, seq, d), ys.reshape(dec_batch, dec_seq, d), k_ctx, v_ctx)
```

```python
import functools
import math

import jax
import jax.numpy as jnp
from jax import lax
from jax.experimental import pallas as pl
from jax.experimental.pallas import tpu as pltpu

F32 = jnp.float32
BF16 = jnp.bfloat16

EPS = 1e-6
ROPE_THETA = 10000.0
GRID_W = 64
N_MIXERS = 3
N_FOURIER_GROUPS = 8
TOP_K = 2
LANES = 128
ROUTE_LANES = 128
VMEM_LIMIT_MB = 60


def _tile(n, pref, mult):
    t = min(pref, n)
    t -= t % mult
    while t >= mult:
        if n % t == 0:
            return t
        t -= mult
    return n


def _params(n_axes, vmem_mb=VMEM_LIMIT_MB):
    return pltpu.CompilerParams(dimension_semantics=("arbitrary",) * n_axes,
                                vmem_limit_bytes=vmem_mb * 1024 * 1024)


def _bdot(a, b):
    return jnp.dot(a.astype(BF16), b.astype(BF16), preferred_element_type=F32)


class _Stream:
    def __init__(self, batch, seq, dec_batch, dec_seq):
        self.batch, self.seq, self.dec_batch, self.dec_seq = batch, seq, dec_batch, dec_seq
        self.n_ctx = batch * seq
        self.n_lat = dec_batch * dec_seq
        self.n_tok = self.n_ctx + self.n_lat
        self.tm = _tile(math.gcd(self.n_ctx, dec_seq), 1024, 16)

    def mod_row(self, i, tm):
        r0 = i * tm
        return jnp.where(r0 < self.n_ctx, 0, 1 + (r0 - self.n_ctx) // self.dec_seq)

    def lat_tile(self, i, tm):
        t0 = self.n_ctx // tm
        return jnp.where(i >= t0, (i - t0) % (self.dec_seq // tm), 0)


def _mod_spec(st, layer, which, tm, width, col_axis):
    if col_axis:
        return pl.BlockSpec((None, None, None, 1, width),
                            lambda i, j: (layer, st.mod_row(i, tm), which, 0, j))
    return pl.BlockSpec((None, None, None, 1, width),
                        lambda i: (layer, st.mod_row(i, tm), which, 0, 0))


def _resid_specs(st, y, tm, tn, two_axes):
    if not isinstance(y, (tuple, list)):
        return [pl.BlockSpec((tm, tn), (lambda i, j: (i, j)) if two_axes else (lambda i: (i, 0)))], [y]
    t0 = st.n_ctx // tm
    if two_axes:
        maps = [lambda i, j: (jnp.minimum(i, t0 - 1), j), lambda i, j: (jnp.maximum(i - t0, 0), j)]
    else:
        maps = [lambda i: (jnp.minimum(i, t0 - 1), 0), lambda i: (jnp.maximum(i - t0, 0), 0)]
    return [pl.BlockSpec((tm, tn), m) for m in maps], list(y)


def _resid_tile(y_refs, ctx_tiles):
    if len(y_refs) == 1:
        return y_refs[0][...]
    return jnp.where(pl.program_id(0) < ctx_tiles, y_refs[0][...], y_refs[1][...])


def _adaln_kernel(c_ref, w_ref, b_ref, o_ref):
    c = c_ref[...]
    s = c * jax.nn.sigmoid(c)
    o_ref[...] = _bdot(s, w_ref[...]) + b_ref[...]


def _adaln_table(cvec8, mod_w, mod_b):
    depth, d, n6 = mod_w.shape
    tn = _tile(n6, 1024, LANES)
    out = pl.pallas_call(
        _adaln_kernel,
        grid=(depth, n6 // tn),
        in_specs=[pl.BlockSpec((8, d), lambda l, j: (0, 0)),
                  pl.BlockSpec((None, d, tn), lambda l, j: (l, 0, j)),
                  pl.BlockSpec((None, 1, tn), lambda l, j: (l, 0, j))],
        out_specs=pl.BlockSpec((None, 8, tn), lambda l, j: (l, 0, j)),
        out_shape=jax.ShapeDtypeStruct((depth, 8, n6), F32),
        compiler_params=_params(2),
        name="adaln_table",
    )(cvec8, mod_w, mod_b.reshape(depth, 1, n6))
    return out.reshape(depth, 8, 6, 1, d)


def _norm_mod_kernel(*refs, ctx_tiles):
    g_ref, sh_ref, sc_ref, o_ref = refs[-4:]
    y = _resid_tile(refs[:-4], ctx_tiles)
    h = y * lax.rsqrt(jnp.mean(y * y, axis=-1, keepdims=True) + EPS) * g_ref[...]
    o_ref[...] = (h * (1.0 + sc_ref[...]) + sh_ref[...]).astype(o_ref.dtype)


def _norm_mod_route_kernel(y_ref, g_ref, sh_ref, sc_ref, rw_ref, o_ref, r_ref, *, n_experts):
    y = y_ref[...]
    h = y * lax.rsqrt(jnp.mean(y * y, axis=-1, keepdims=True) + EPS) * g_ref[...]
    h = h * (1.0 + sc_ref[...]) + sh_ref[...]
    o_ref[...] = h
    logits = jnp.dot(h, rw_ref[...], preferred_element_type=F32,
                     precision=lax.Precision.HIGHEST)
    lane = lax.broadcasted_iota(jnp.int32, logits.shape, 1).astype(F32)
    neg = jnp.float32(-jnp.inf)
    lg = jnp.where(lane < n_experts, logits, neg)
    m1 = jnp.max(lg, axis=-1, keepdims=True)
    i1 = jnp.min(jnp.where(lg == m1, lane, float(ROUTE_LANES)), axis=-1, keepdims=True)
    lg2 = jnp.where(lane == i1, neg, lg)
    m2 = jnp.max(lg2, axis=-1, keepdims=True)
    i2 = jnp.min(jnp.where(lg2 == m2, lane, float(ROUTE_LANES)), axis=-1, keepdims=True)
    e = jnp.exp(m2 - m1)
    g1 = 1.0 / (1.0 + e)
    g2 = e / (1.0 + e)
    r_ref[...] = jnp.where(lane == 0, i1, jnp.where(lane == 1, i2,
                           jnp.where(lane == 2, g1, jnp.where(lane == 3, g2, 0.0))))


def _norm_mod(st, y, norm_g4, mods, layer, sub, out_dtype, router_w_pad=None, n_experts=0):
    n, d = st.n_tok, norm_g4.shape[-1]
    tm = _tile(st.tm, 512, 16)
    y_specs, y_args = _resid_specs(st, y, tm, d, False)
    in_specs = y_specs + [pl.BlockSpec((None, None, 1, d), lambda i: (layer, sub, 0, 0)),
                          _mod_spec(st, layer, 3 * sub + 0, tm, d, False),
                          _mod_spec(st, layer, 3 * sub + 1, tm, d, False)]
    if router_w_pad is None:
        return pl.pallas_call(
            functools.partial(_norm_mod_kernel, ctx_tiles=st.n_ctx // tm),
            grid=(n // tm,), in_specs=in_specs,
            out_specs=pl.BlockSpec((tm, d), lambda i: (i, 0)),
            out_shape=jax.ShapeDtypeStruct((n, d), out_dtype),
            compiler_params=_params(1), name="norm_mod",
        )(*y_args, norm_g4, mods, mods)
    (y,) = y_args
    in_specs.append(pl.BlockSpec((d, ROUTE_LANES), lambda i: (0, 0)))
    return pl.pallas_call(
        functools.partial(_norm_mod_route_kernel, n_experts=n_experts),
        grid=(n // tm,), in_specs=in_specs,
        out_specs=[pl.BlockSpec((tm, d), lambda i: (i, 0)),
                   pl.BlockSpec((tm, ROUTE_LANES), lambda i: (i, 0))],
        out_shape=[jax.ShapeDtypeStruct((n, d), F32),
                   jax.ShapeDtypeStruct((n, ROUTE_LANES), F32)],
        compiler_params=_params(1), name="norm_mod_route",
    )(y, norm_g4, mods, mods, router_w_pad)


def _mm_plain_kernel(x_ref, w_ref, *rest, has_bias):
    acc = _bdot(x_ref[...], w_ref[...])
    outs = rest
    if has_bias:
        acc = acc + rest[0][...]
        outs = rest[1:]
    for o_ref in outs:
        o_ref[...] = acc.astype(o_ref.dtype)


def _mm_plain(x, w3, wi, *, tm, tn, out_dtypes, bias3=None, block_diag=False):
    m = x.shape[0]
    kb, n_w = w3.shape[1], w3.shape[2]
    if block_diag:
        n_out, tn = x.shape[1], n_w
        x_spec = pl.BlockSpec((tm, kb), lambda i, j: (i, j))
        w_spec = pl.BlockSpec((None, kb, n_w), lambda i, j: (wi, 0, 0))
    else:
        n_out = n_w
        x_spec = pl.BlockSpec((tm, kb), lambda i, j: (i, 0))
        w_spec = pl.BlockSpec((None, kb, tn), lambda i, j: (wi, 0, j))
    in_specs, args = [x_spec, w_spec], [x, w3]
    if bias3 is not None:
        in_specs.append(pl.BlockSpec((None, 1, tn), lambda i, j: (wi, 0, j)))
        args.append(bias3)
    outs = pl.pallas_call(
        functools.partial(_mm_plain_kernel, has_bias=bias3 is not None),
        grid=(m // tm, n_out // tn), in_specs=in_specs,
        out_specs=[pl.BlockSpec((tm, tn), lambda i, j: (i, j)) for _ in out_dtypes],
        out_shape=[jax.ShapeDtypeStruct((m, n_out), dt) for dt in out_dtypes],
        compiler_params=_params(2), name="mm_plain",
    )(*args)
    return outs


def _mm_glu_kernel(x_ref, wa_ref, wb_ref, *rest, has_bias, act):
    x = x_ref[...]
    a = _bdot(x, wa_ref[...])
    b = _bdot(x, wb_ref[...])
    if has_bias:
        a = a + rest[0][...]
        b = b + rest[1][...]
    o_ref = rest[-1]
    if act == "swiglu":
        o_ref[...] = (a * jax.nn.sigmoid(a) * b).astype(o_ref.dtype)
    else:
        o_ref[...] = (a * jax.nn.sigmoid(b)).astype(o_ref.dtype)


def _mm_glu(x, wa3, wb3, wi, *, n_out, b_col_off, tm, tn, act, out_dtype, bias3=None, x_buffers=2):
    m, k = x.shape
    off = b_col_off // tn
    x_mode = {} if x_buffers == 2 else dict(pipeline_mode=pl.Buffered(x_buffers))
    in_specs = [pl.BlockSpec((tm, k), lambda i, j: (i, 0), **x_mode),
                pl.BlockSpec((None, k, tn), lambda i, j: (wi, 0, j)),
                pl.BlockSpec((None, k, tn), lambda i, j: (wi, 0, j + off))]
    args = [x, wa3, wb3]
    if bias3 is not None:
        in_specs += [pl.BlockSpec((None, 1, tn), lambda i, j: (wi, 0, j)),
                     pl.BlockSpec((None, 1, tn), lambda i, j: (wi, 0, j + off))]
        args += [bias3, bias3]
    return pl.pallas_call(
        functools.partial(_mm_glu_kernel, has_bias=bias3 is not None, act=act),
        grid=(m // tm, n_out // tn), in_specs=in_specs,
        out_specs=pl.BlockSpec((tm, tn), lambda i, j: (i, j)),
        out_shape=jax.ShapeDtypeStruct((m, n_out), out_dtype),
        compiler_params=_params(2), name="mm_glu",
    )(*args)


def _mm_resid_kernel(x_ref, w_ref, *rest, has_bias, ctx_tiles):
    acc = _bdot(x_ref[...], w_ref[...])
    if has_bias:
        acc = acc + rest[0][...]
        rest = rest[1:]
    gate_ref, o_ref = rest[-2:]
    o_ref[...] = _resid_tile(rest[:-2], ctx_tiles) + gate_ref[...] * acc


def _mm_resid(st, x, w3, wi, y, mods, layer, which_gate, *, tm, tn, bias3=None):
    m, k = x.shape
    n = w3.shape[2]
    in_specs = [pl.BlockSpec((tm, k), lambda i, j: (i, 0)),
                pl.BlockSpec((None, k, tn), lambda i, j: (wi, 0, j))]
    args = [x, w3]
    if bias3 is not None:
        in_specs.append(pl.BlockSpec((None, 1, tn), lambda i, j: (wi, 0, j)))
        args.append(bias3)
    y_specs, y_args = _resid_specs(st, y, tm, tn, True)
    in_specs += y_specs + [_mod_spec(st, layer, which_gate, tm, tn, True)]
    args += y_args + [mods]
    return pl.pallas_call(
        functools.partial(_mm_resid_kernel, has_bias=bias3 is not None, ctx_tiles=st.n_ctx // tm),
        grid=(m // tm, n // tn), in_specs=in_specs,
        out_specs=pl.BlockSpec((tm, tn), lambda i, j: (i, j)),
        out_shape=jax.ShapeDtypeStruct((m, n), F32),
        compiler_params=_params(2), name="mm_resid",
    )(*args)


def _headnorm_step(x_ref, w_ref, g_ref, cos_ref, sin_ref, ob_ref, of_ref, acc_new, acc_old):
    tm, tn = ob_ref.shape
    lane = lax.broadcasted_iota(jnp.int32, (tm, LANES), 1)
    low = (lane % (LANES // 2)) < (LANES // 4)
    g = g_ref[...]
    c = cos_ref[...]
    s = sin_ref[...]
    for hh in range(tn // LANES):
        cols = slice(hh * LANES, (hh + 1) * LANES)
        a = acc_old[:, cols]
        r = a * lax.rsqrt(jnp.mean(a * a, axis=-1, keepdims=True) + EPS) * g
        if of_ref is not None:
            of_ref[:, cols] = r
        partner = jnp.where(low, pltpu.roll(r, LANES - LANES // 4, 1), pltpu.roll(r, LANES // 4, 1))
        ob_ref[:, cols] = (r * c + partner * s).astype(ob_ref.dtype)
    acc_new[...] = _bdot(x_ref[...], w_ref[...])


def _mm_headnorm_kernel(x_ref, w_ref, g_ref, cos_ref, sin_ref, ob_ref, *rest):
    of_ref = rest[0] if len(rest) == 3 else None
    acc0, acc1 = rest[-2:]
    t = pl.program_id(0)

    @pl.when(t == 0)
    def _():
        acc1[...] = jnp.zeros_like(acc1)

    step = functools.partial(_headnorm_step, x_ref, w_ref, g_ref, cos_ref, sin_ref, ob_ref, of_ref)
    pl.when(t % 2 == 0)(functools.partial(step, acc0, acc1))
    pl.when(t % 2 == 1)(functools.partial(step, acc1, acc0))


def _mm_headnorm(st, x, w3, wi, g3, cos, sin, *, tm, tn, want_f32):
    m, k = x.shape
    n = w3.shape[2]
    nj = n // tn
    n_tiles = (m // tm) * nj
    ident = st.dec_seq // tm
    cur = lambda t: jnp.minimum(t, n_tiles - 1)
    prv = lambda t: jnp.maximum(t - 1, 0)

    def tab(t):
        i = prv(t) // nj
        return (jnp.where(i >= st.n_ctx // tm, st.lat_tile(i, tm), ident), 0)

    out_idx = lambda t: (prv(t) // nj, prv(t) % nj)
    out_specs = [pl.BlockSpec((tm, tn), out_idx)]
    out_shape = [jax.ShapeDtypeStruct((m, n), BF16)]
    if want_f32:
        out_specs.append(pl.BlockSpec((tm, tn), out_idx))
        out_shape.append(jax.ShapeDtypeStruct((m, n), F32))
    return pl.pallas_call(
        _mm_headnorm_kernel,
        grid=(n_tiles + 1,),
        in_specs=[pl.BlockSpec((tm, k), lambda t: (cur(t) // nj, 0)),
                  pl.BlockSpec((None, k, tn), lambda t: (wi, 0, cur(t) % nj)),
                  pl.BlockSpec((None, 1, LANES), lambda t: (wi, 0, 0)),
                  pl.BlockSpec((tm, LANES), tab),
                  pl.BlockSpec((tm, LANES), tab)],
        out_specs=out_specs, out_shape=out_shape,
        scratch_shapes=[pltpu.VMEM((tm, tn), F32), pltpu.VMEM((tm, tn), F32)],
        compiler_params=_params(1), name="mm_headnorm",
    )(x, w3, g3, cos, sin)


def _rope_tables(dec_seq, head_dim, tm, gain):
    half = head_dim // 2
    t = jnp.arange(dec_seq)
    rows = (t // GRID_W).astype(F32)
    cols = (t % GRID_W).astype(F32)
    inv = ROPE_THETA ** (-jnp.arange(0, half, 2, dtype=F32) / half)
    ar = rows[:, None] * inv
    ac = cols[:, None] * inv
    cos = jnp.concatenate([jnp.cos(ar), jnp.cos(ar), jnp.cos(ac), jnp.cos(ac)], axis=-1)
    sin = jnp.concatenate([-jnp.sin(ar), jnp.sin(ar), -jnp.sin(ac), jnp.sin(ac)], axis=-1)
    cos = jnp.concatenate([cos, jnp.ones((tm, head_dim), F32)], axis=0)
    sin = jnp.concatenate([sin, jnp.zeros((tm, head_dim), F32)], axis=0)
    return cos * gain, sin * gain


ATTN_KEY_CHUNK = 1024


def _flash(q4, kvs):
    dn = (((1,), (1,)), ((), ()))
    m = l = acc = None
    for k, v in kvs:
        s = lax.dot_general(q4, k, dn, preferred_element_type=F32)
        mc = s.max(axis=-1, keepdims=True)
        if m is None:
            m_new = mc
            p = jnp.exp2(s - m_new)
            l = p.sum(axis=-1, keepdims=True)
            acc = jnp.dot(p.astype(BF16), v, preferred_element_type=F32)
        else:
            m_new = jnp.maximum(m, mc)
            alpha = jnp.exp2(m - m_new)
            p = jnp.exp2(s - m_new)
            l = alpha * l + p.sum(axis=-1, keepdims=True)
            acc = alpha * acc + jnp.dot(p.astype(BF16), v, preferred_element_type=F32)
        m = m_new
    return acc / l


def _attn_ctx_kernel(q_ref, k_ref, v_ref, o_ref, *, n_seq, n_kv, group, head_dim):
    @pl.when(pl.program_id(0) >= n_seq)
    def _():
        o_ref[...] = jnp.zeros_like(o_ref)

    pl.when(pl.program_id(0) < n_seq)(
        functools.partial(_attn_ctx_body, q_ref, k_ref, v_ref, o_ref, n_kv, group, head_dim))


def _attn_ctx_body(q_ref, k_ref, v_ref, o_ref, n_kv, group, head_dim):
    tq = q_ref.shape[0]
    for kv in range(n_kv):
        kh = k_ref[:, kv * head_dim:(kv + 1) * head_dim]
        vh = v_ref[:, kv * head_dim:(kv + 1) * head_dim]
        c0 = kv * group * head_dim
        q4 = jnp.concatenate([q_ref[:, c0 + g * head_dim:c0 + (g + 1) * head_dim]
                              for g in range(group)], axis=0)
        o = _flash(q4, [(kh, vh)])
        for g in range(group):
            o_ref[:, c0 + g * head_dim:c0 + (g + 1) * head_dim] = (
                o[g * tq:(g + 1) * tq].astype(o_ref.dtype))


def _attn_lat_kernel(q_ref, k_ref, v_ref, ck_ref, cv_ref, prev_ref, o_ref, *, group, head_dim, chunk):
    del prev_ref
    tq = q_ref.shape[0]
    q4 = jnp.concatenate([q_ref[:, g * head_dim:(g + 1) * head_dim] for g in range(group)], axis=0)
    kvs = [(ck_ref[...].astype(BF16), cv_ref[...].astype(BF16))]
    for c0 in range(0, k_ref.shape[0], chunk):
        kvs.append((k_ref[c0:c0 + chunk, :], v_ref[c0:c0 + chunk, :]))
    o = _flash(q4, kvs)
    for g in range(group):
        o_ref[:, g * head_dim:(g + 1) * head_dim] = o[g * tq:(g + 1) * tq].astype(o_ref.dtype)


def _attention(st, q, k, v, cache_k3, cache_v3, n_kv, head_dim):
    n, d = q.shape
    kvd = n_kv * head_dim
    group = d // kvd
    gw = group * head_dim
    ctx = pl.pallas_call(
        functools.partial(_attn_ctx_kernel, n_seq=st.batch, n_kv=n_kv, group=group, head_dim=head_dim),
        grid=(n // st.seq,),
        in_specs=[pl.BlockSpec((st.seq, d), lambda b: (jnp.minimum(b, st.batch - 1), 0)),
                  pl.BlockSpec((st.seq, kvd), lambda b: (jnp.minimum(b, st.batch - 1), 0)),
                  pl.BlockSpec((st.seq, kvd), lambda b: (jnp.minimum(b, st.batch - 1), 0))],
        out_specs=pl.BlockSpec((st.seq, d), lambda b: (b, 0)),
        out_shape=jax.ShapeDtypeStruct((n, d), BF16),
        compiler_params=_params(1), name="attn_ctx",
    )(q, k, v)
    tq = _tile(st.dec_seq, 256, 16)
    qt = st.dec_seq // tq
    q0 = st.n_ctx // tq
    s0 = st.n_ctx // st.dec_seq
    past = cache_k3.shape[1]
    chunk = _tile(st.dec_seq, ATTN_KEY_CHUNK, LANES)
    return pl.pallas_call(
        functools.partial(_attn_lat_kernel, group=group, head_dim=head_dim, chunk=chunk),
        grid=(st.dec_batch, n_kv, qt),
        in_specs=[pl.BlockSpec((tq, gw), lambda b, h, i: (q0 + b * qt + i, h)),
                  pl.BlockSpec((st.dec_seq, head_dim), lambda b, h, i: (s0 + b, h)),
                  pl.BlockSpec((st.dec_seq, head_dim), lambda b, h, i: (s0 + b, h)),
                  pl.BlockSpec((None, past, head_dim), lambda b, h, i: (b, 0, h)),
                  pl.BlockSpec((None, past, head_dim), lambda b, h, i: (b, 0, h)),
                  pl.BlockSpec(memory_space=pl.ANY)],
        out_specs=pl.BlockSpec((tq, gw), lambda b, h, i: (q0 + b * qt + i, h)),
        out_shape=jax.ShapeDtypeStruct((n, d), BF16),
        input_output_aliases={5: 0},
        compiler_params=_params(3), name="attn_lat",
    )(q, k, v, cache_k3, cache_v3, ctx)


CONV_HALO = 16
F32_SUBLANES = 8
LN_ROWS = 16


def _conv_ln_kernel(cur_ref, prev_ref, next_ref, dw_ref, dwb_ref, g_ref, b_ref, o_ref, xp_ref, cv_ref,
                    sh_ref, *, st, tc, width, cw):
    i = pl.program_id(0)
    r0 = i * tc
    in_ctx = r0 < st.n_ctx
    pos = jnp.where(in_ctx, r0 % st.seq, (r0 - st.n_ctx) % st.dec_seq)
    seq_len = jnp.where(in_ctx, st.seq, st.dec_seq)
    first = pos == 0
    last = pos + tc == seq_len
    d = cur_ref.shape[1]
    xp_ref[0:CONV_HALO, :] = jnp.where(first, 0.0, prev_ref[...])
    xp_ref[CONV_HALO:CONV_HALO + tc, :] = cur_ref[...]
    xp_ref[CONV_HALO + tc:2 * CONV_HALO + tc, :] = jnp.where(last, 0.0, next_ref[...])
    base = CONV_HALO - width // 2

    n_a = -(-width // F32_SUBLANES)
    span = sh_ref.shape[1]

    def chunk(c, carry):
        off = pl.multiple_of(c * cw, cw)
        acc = jnp.zeros((tc, cw), F32)
        for b in range(F32_SUBLANES):
            start = base + b
            aligned = start % F32_SUBLANES == 0
            if not aligned:
                sh_ref[b] = xp_ref[pl.ds(start, span), pl.ds(off, cw)]
            for a in range(n_a):
                j = F32_SUBLANES * a + b
                if j >= width:
                    continue
                if aligned:
                    rows = xp_ref[pl.ds(start + F32_SUBLANES * a, tc), pl.ds(off, cw)]
                else:
                    rows = sh_ref[b, pl.ds(F32_SUBLANES * a, tc), :]
                acc = acc + rows * dw_ref[pl.ds(j, 1), pl.ds(off, cw)]
        cv_ref[:, pl.ds(off, cw)] = acc + dwb_ref[:, pl.ds(off, cw)]
        return carry

    lax.fori_loop(0, d // cw, chunk, 0)

    def ln_rows(rb, carry):
        rows = pl.ds(pl.multiple_of(rb * LN_ROWS, LN_ROWS), LN_ROWS)
        u = cv_ref[rows, :]
        mu = jnp.mean(u, axis=-1, keepdims=True)
        uc = u - mu
        var = jnp.mean(uc * uc, axis=-1, keepdims=True)
        yv = uc * lax.rsqrt(var + EPS) * g_ref[...] + b_ref[...]
        o_ref[rows, :] = (yv * jax.nn.sigmoid(yv)).astype(o_ref.dtype)
        return carry

    lax.fori_loop(0, tc // LN_ROWS, ln_rows, 0, unroll=4)


def _conv_ln(st, u, dw3, dwb3, lng3, lnb3, wi):
    n, d = u.shape
    width = dw3.shape[1]
    tc = _tile(math.gcd(st.seq, st.dec_seq), 256, CONV_HALO)
    hb = tc // CONV_HALO
    n_halo = n // CONV_HALO
    cw = LANES
    span = tc + F32_SUBLANES * (-(-width // F32_SUBLANES) - 1)
    vec = lambda: pl.BlockSpec((None, 1, d), lambda i: (wi, 0, 0))
    return pl.pallas_call(
        functools.partial(_conv_ln_kernel, st=st, tc=tc, width=width, cw=cw),
        grid=(n // tc,),
        in_specs=[pl.BlockSpec((tc, d), lambda i: (i, 0)),
                  pl.BlockSpec((CONV_HALO, d), lambda i: (jnp.maximum(i * hb - 1, 0), 0)),
                  pl.BlockSpec((CONV_HALO, d), lambda i: (jnp.minimum((i + 1) * hb, n_halo - 1), 0)),
                  pl.BlockSpec((None, width, d), lambda i: (wi, 0, 0)),
                  vec(), vec(), vec()],
        out_specs=pl.BlockSpec((tc, d), lambda i: (i, 0)),
        out_shape=jax.ShapeDtypeStruct((n, d), BF16),
        scratch_shapes=[pltpu.VMEM((tc + 2 * CONV_HALO, d), F32), pltpu.VMEM((tc, d), F32),
                        pltpu.VMEM((F32_SUBLANES, span, cw), F32)],
        compiler_params=_params(1), name="conv_ln",
    )(u, u, u, dw3, dwb3, lng3, lnb3)


def _dft_mats(n):
    nb = 1 << (n.bit_length() // 2)
    na = n // nb
    k = jnp.arange(n, dtype=jnp.int32)

    def thin(step, rows):
        ang = ((jnp.arange(rows, dtype=jnp.int32)[:, None] * step * k[None, :]) % n).astype(F32)
        ang = ang * (2.0 * math.pi / n)
        return jnp.cos(ang), jnp.sin(ang)

    ca, sa = (t[:, None, :] for t in thin(nb, na))
    cb, sb = (t[None, :, :] for t in thin(1, nb))
    cos = (ca * cb - sa * sb).reshape(n, n)
    sin = (sa * cb + ca * sb).reshape(n, n)
    return cos.astype(BF16), sin.astype(BF16)


def _dft_time_kernel(ct_ref, st_ref, xc_ref, xs_ref, *rest, scale, n_seq):
    o_ref = rest[-1]

    @pl.when(pl.program_id(1) < n_seq)
    def _():
        acc = (jnp.dot(ct_ref[...], xc_ref[...], preferred_element_type=F32)
               - jnp.dot(st_ref[...], xs_ref[...], preferred_element_type=F32))
        o_ref[...] = (acc * scale).astype(o_ref.dtype)

    @pl.when(pl.program_id(1) >= n_seq)
    def _():
        o_ref[...] = jnp.zeros_like(o_ref)


def _dft_time(xc, xs, n_seq, t, row0, scale, prev=None):
    n, d = xc.shape
    ct, sn = _dft_mats(t)
    tr = _tile(t, 512, 16)
    tn = _tile(d, 512 if t > 1024 else 2048, LANES)
    s0 = row0 // t
    r0 = row0 // tr
    rt = t // tr
    n_steps = n_seq if prev is not None else (n - row0) // t
    in_specs = [pl.BlockSpec((tr, t), lambda i, b, j: (i, 0)),
                pl.BlockSpec((tr, t), lambda i, b, j: (i, 0)),
                pl.BlockSpec((t, tn), lambda i, b, j: (s0 + jnp.minimum(b, n_seq - 1), j)),
                pl.BlockSpec((t, tn), lambda i, b, j: (s0 + jnp.minimum(b, n_seq - 1), j))]
    args = [ct, sn, xc, xs]
    aliases = {}
    if prev is not None:
        in_specs.append(pl.BlockSpec(memory_space=pl.ANY))
        args.append(prev)
        aliases = {4: 0}
    return pl.pallas_call(
        functools.partial(_dft_time_kernel, scale=scale, n_seq=n_seq),
        grid=(rt, n_steps, d // tn), in_specs=in_specs,
        out_specs=pl.BlockSpec((tr, tn), lambda i, b, j: (r0 + b * rt + i, j)),
        out_shape=jax.ShapeDtypeStruct((n, d), BF16),
        input_output_aliases=aliases,
        compiler_params=_params(3), name="dft_time",
    )(*args)


def _fourier(st, h):
    n, d = h.shape
    gsz = d // N_FOURIER_GROUPS
    cc, sc = _dft_mats(gsz)
    tm = _tile(n, 2048, 16)
    xc, = _mm_plain(h, cc[None], 0, tm=tm, tn=gsz, out_dtypes=[BF16], block_diag=True)
    xs, = _mm_plain(h, sc[None], 0, tm=tm, tn=gsz, out_dtypes=[BF16], block_diag=True)
    f = _dft_time(xc, xs, st.batch, st.seq, 0, (st.seq * gsz) ** -0.5)
    return _dft_time(xc, xs, st.dec_batch, st.dec_seq, st.n_ctx, (st.dec_seq * gsz) ** -0.5, prev=f)


DMA_ISSUE_UNROLL = 8
DMA_PRIORITIES = 2


def _gather_rows_kernel(src_ref, nxt_ref, h_ref, o_ref, buf_ref, sem):
    tg = buf_ref.shape[1]
    i = pl.program_id(0)
    slot = i % 2

    def issue_rows(idx_ref, dst_slot):
        def body(rp, carry):
            for prio in range(DMA_PRIORITIES):
                r = DMA_PRIORITIES * rp + prio
                pltpu.make_async_copy(h_ref.at[pl.ds(idx_ref[0, r], 1)], buf_ref.at[dst_slot, pl.ds(r, 1)],
                                      sem.at[dst_slot]).start(priority=prio)
            return carry
        lax.fori_loop(0, tg // DMA_PRIORITIES, body, 0, unroll=DMA_ISSUE_UNROLL)

    @pl.when(i == 0)
    def _():
        issue_rows(src_ref, 0)

    @pl.when(i + 1 < pl.num_programs(0))
    def _():
        issue_rows(nxt_ref, 1 - slot)

    pltpu.make_async_copy(h_ref.at[pl.ds(0, tg)], buf_ref.at[slot], sem.at[slot]).wait()
    o_ref[...] = buf_ref[slot].astype(o_ref.dtype)


def _gather_rows(h, src, tg):
    n, d = h.shape
    p = src.shape[0]
    steps = p // tg
    src3 = src.reshape(steps, 1, tg)
    return pl.pallas_call(
        _gather_rows_kernel,
        grid=(steps,),
        in_specs=[pl.BlockSpec((None, 1, tg), lambda i: (i, 0, 0), memory_space=pltpu.SMEM),
                  pl.BlockSpec((None, 1, tg), lambda i: (jnp.minimum(i + 1, steps - 1), 0, 0),
                               memory_space=pltpu.SMEM),
                  pl.BlockSpec(memory_space=pl.ANY)],
        out_specs=pl.BlockSpec((tg, d), lambda i: (i, 0)),
        out_shape=jax.ShapeDtypeStruct((p, d), BF16),
        scratch_shapes=[pltpu.VMEM((2, tg, d), F32), pltpu.SemaphoreType.DMA((2,))],
        compiler_params=_params(1), name="moe_gather",
    )(src3, src3, h)


def _moe_up_kernel(te_ref, tv_ref, x_ref, w1_ref, w3_ref, o_ref):
    i = pl.program_id(1)

    @pl.when(tv_ref[i] == 1)
    def _():
        x = x_ref[...]
        a = _bdot(x, w1_ref[...])
        b = _bdot(x, w3_ref[...])
        o_ref[...] = (a * jax.nn.sigmoid(a) * b).astype(o_ref.dtype)

    @pl.when(tv_ref[i] == 0)
    def _():
        o_ref[...] = jnp.zeros_like(o_ref)


def _moe_down_kernel(te_ref, tv_ref, x_ref, w_ref, o_ref):
    i = pl.program_id(1)

    @pl.when(tv_ref[i] == 1)
    def _():
        o_ref[...] = _bdot(x_ref[...], w_ref[...])

    @pl.when(tv_ref[i] == 0)
    def _():
        o_ref[...] = jnp.zeros_like(o_ref)


def _moe_experts(xs, tile_e, tile_valid, w1, w3, w2, f, tm):
    p, d = xs.shape
    ff = w1.shape[3]
    tn = _tile(ff, 512, LANES)
    g = pl.pallas_call(
        _moe_up_kernel,
        grid_spec=pltpu.PrefetchScalarGridSpec(
            num_scalar_prefetch=2, grid=(ff // tn, p // tm),
            in_specs=[pl.BlockSpec((tm, d), lambda j, i, te, tv: (i, 0)),
                      pl.BlockSpec((None, None, d, tn), lambda j, i, te, tv: (f, te[i], 0, j)),
                      pl.BlockSpec((None, None, d, tn), lambda j, i, te, tv: (f, te[i], 0, j))],
            out_specs=pl.BlockSpec((tm, tn), lambda j, i, te, tv: (i, j))),
        out_shape=jax.ShapeDtypeStruct((p, ff), BF16),
        compiler_params=_params(2), name="moe_up",
    )(tile_e, tile_valid, xs, w1, w3)
    tn2 = _tile(d, 1024, LANES)
    return pl.pallas_call(
        _moe_down_kernel,
        grid_spec=pltpu.PrefetchScalarGridSpec(
            num_scalar_prefetch=2, grid=(d // tn2, p // tm),
            in_specs=[pl.BlockSpec((tm, ff), lambda j, i, te, tv: (i, 0)),
                      pl.BlockSpec((None, None, ff, tn2), lambda j, i, te, tv: (f, te[i], 0, j))],
            out_specs=pl.BlockSpec((tm, tn2), lambda j, i, te, tv: (i, j))),
        out_shape=jax.ShapeDtypeStruct((p, d), F32),
        compiler_params=_params(2), name="moe_down",
    )(tile_e, tile_valid, g, w2)


def _moe_combine_kernel(pos_ref, nxt_ref, route_ref, y_ref, gate_ref, ye_ref, *rest, n_ctx_tiles):
    outs, (buf_ref, sem) = rest[:-2], rest[-2:]
    tc = buf_ref.shape[2]
    i = pl.program_id(0)
    slot = i % 2

    def issue_rows(idx_ref, dst_slot):
        def body(r, carry):
            for k in range(TOP_K):
                pltpu.make_async_copy(ye_ref.at[pl.ds(idx_ref[0, TOP_K * r + k], 1)],
                                      buf_ref.at[dst_slot, k, pl.ds(r, 1)],
                                      sem.at[dst_slot]).start(priority=k % DMA_PRIORITIES)
            return carry
        lax.fori_loop(0, tc, body, 0, unroll=DMA_ISSUE_UNROLL)

    @pl.when(i == 0)
    def _():
        issue_rows(pos_ref, 0)

    @pl.when(i + 1 < pl.num_programs(0))
    def _():
        issue_rows(nxt_ref, 1 - slot)

    for k in range(TOP_K):
        pltpu.make_async_copy(ye_ref.at[pl.ds(0, tc)], buf_ref.at[slot, k], sem.at[slot]).wait()
    route = route_ref[...]
    f = route[:, 2:3] * buf_ref[slot, 0] + route[:, 3:4] * buf_ref[slot, 1]
    res = y_ref[...] + gate_ref[...] * f
    if len(outs) == 1:
        outs[0][...] = res
    else:
        @pl.when(i < n_ctx_tiles)
        def _():
            outs[0][...] = res

        @pl.when(i >= n_ctx_tiles)
        def _():
            outs[1][...] = res


def _moe_combine(st, y, ye, pos, route, mods, layer, which_gate, split):
    n, d = y.shape
    tc = _tile(st.tm, 128, 8)
    steps = n // tc
    n_ctx_tiles = st.n_ctx // tc
    pos3 = pos.reshape(steps, 1, TOP_K * tc)
    if split:
        out_specs = [pl.BlockSpec((tc, d), lambda i: (jnp.minimum(i, n_ctx_tiles - 1), 0)),
                     pl.BlockSpec((tc, d), lambda i: (jnp.maximum(i - n_ctx_tiles, 0), 0))]
        out_shape = [jax.ShapeDtypeStruct((st.n_ctx, d), F32), jax.ShapeDtypeStruct((st.n_lat, d), F32)]
    else:
        out_specs = [pl.BlockSpec((tc, d), lambda i: (i, 0))]
        out_shape = [jax.ShapeDtypeStruct((n, d), F32)]
    return pl.pallas_call(
        functools.partial(_moe_combine_kernel, n_ctx_tiles=n_ctx_tiles),
        grid=(steps,),
        in_specs=[pl.BlockSpec((None, 1, TOP_K * tc), lambda i: (i, 0, 0), memory_space=pltpu.SMEM),
                  pl.BlockSpec((None, 1, TOP_K * tc), lambda i: (jnp.minimum(i + 1, steps - 1), 0, 0),
                               memory_space=pltpu.SMEM),
                  pl.BlockSpec((tc, ROUTE_LANES), lambda i: (i, 0)),
                  pl.BlockSpec((tc, d), lambda i: (i, 0)),
                  _mod_spec(st, layer, which_gate, tc, d, False),
                  pl.BlockSpec(memory_space=pl.ANY)],
        out_specs=out_specs, out_shape=out_shape,
        scratch_shapes=[pltpu.VMEM((2, TOP_K, tc, d), F32), pltpu.SemaphoreType.DMA((2,))],
        compiler_params=_params(1), name="moe_combine",
    )(pos3, pos3, route, y, mods, ye)


def _moe_plan(route, n_experts, tm):
    n = route.shape[0]
    e_flat = route[:, :TOP_K].astype(jnp.int32).reshape(-1)
    onehot = (e_flat[:, None] == jnp.arange(n_experts, dtype=jnp.int32)[None, :]).astype(jnp.int32)
    csum = jnp.cumsum(onehot, axis=0)
    rank = jnp.sum(onehot * csum, axis=1) - 1
    counts = csum[-1]
    padded = ((counts + tm - 1) // tm) * tm
    ends = jnp.cumsum(padded)
    starts = ends - padded
    pos = jnp.sum(onehot * starts[None, :], axis=1) + rank
    p_rows = TOP_K * n + n_experts * tm
    src = jnp.zeros((p_rows,), jnp.int32).at[pos].set(jnp.arange(TOP_K * n, dtype=jnp.int32) // TOP_K)
    tile_start = jnp.arange(p_rows // tm, dtype=jnp.int32) * tm
    tile_e = jnp.minimum(jnp.sum((tile_start[:, None] >= ends[None, :]).astype(jnp.int32), axis=1),
                         n_experts - 1)
    tile_valid = (tile_start < ends[-1]).astype(jnp.int32)
    return pos.reshape(n, TOP_K), src, tile_e, tile_valid


def kernel(x_prompt, x_sample, cache_k, cache_v, c, c_ctx, mod_w, mod_b, norm_g, wq, wk, wv, wo, q_norm, k_norm, conv_pw1, conv_pw1_b, conv_dw, conv_dw_b, conv_ln_g, conv_ln_b, conv_pw2, conv_pw2_b, fnet_w, fnet_b, ffn_w1, ffn_w3, ffn_w2, router_w, moe_w1, moe_w3, moe_w2):
    batch, seq, d = x_prompt.shape
    dec_batch, dec_seq, _ = x_sample.shape
    depth = mod_w.shape[0]
    n_attn, past, n_kv, head_dim = cache_k.shape[1:]
    n_experts = router_w.shape[2]
    st = _Stream(batch, seq, dec_batch, dec_seq)
    tm = st.tm
    assert st.n_ctx % dec_seq == 0 and dec_batch + 1 <= 8 and head_dim == LANES

    y = (x_prompt.reshape(st.n_ctx, d), x_sample.reshape(st.n_lat, d))
    cvec8 = jnp.concatenate([c_ctx[None, :], c, jnp.zeros((8 - 1 - dec_batch, d), F32)], axis=0)
    mods = _adaln_table(cvec8, mod_w, mod_b)
    norm_g4 = norm_g.reshape(depth, 2, 1, d)
    q_gain = head_dim ** -0.5 * math.log2(math.e)
    cos_q, sin_q = _rope_tables(dec_seq, head_dim, tm, q_gain)
    cos_k, sin_k = _rope_tables(dec_seq, head_dim, tm, 1.0)
    ffn_w2_b = ffn_w2.astype(BF16)
    wq, wk, wv, wo, conv_pw2, fnet_w = (w.astype(BF16) for w in (wq, wk, wv, wo, conv_pw2, fnet_w))

    def vec3(a):
        return a.reshape(a.shape[0], 1, a.shape[1])

    k_list, v_list = [], []
    for layer in range(depth):
        kind = layer % N_MIXERS
        j = layer // N_MIXERS
        h = _norm_mod(st, y, norm_g4, mods, layer, 0, BF16)
        if kind == 0:
            q, = _mm_headnorm(st, h, wq, j, vec3(q_norm), cos_q, sin_q, tm=tm, tn=_tile(d, 512, LANES),
                              want_f32=False)
            kb, kf = _mm_headnorm(st, h, wk, j, vec3(k_norm), cos_k, sin_k, tm=tm,
                                  tn=_tile(n_kv * head_dim, 512, LANES), want_f32=True)
            vb, vf = _mm_plain(h, wv, j, tm=tm, tn=_tile(n_kv * head_dim, 1024, LANES),
                               out_dtypes=[BF16, F32])
            k_list.append(kf[:st.n_ctx].reshape(batch, seq, n_kv, head_dim))
            v_list.append(vf[:st.n_ctx].reshape(batch, seq, n_kv, head_dim))
            a = _attention(st, q, kb, vb,
                           cache_k[:, j].reshape(dec_batch, past, n_kv * head_dim),
                           cache_v[:, j].reshape(dec_batch, past, n_kv * head_dim), n_kv, head_dim)
            y = _mm_resid(st, a, wo, j, y, mods, layer, 2, tm=tm, tn=_tile(d, 512, LANES))
        elif kind == 1:
            u = _mm_glu(h, conv_pw1, conv_pw1, j, n_out=d, b_col_off=d, tm=tm, tn=_tile(d, 256, LANES),
                        act="glu", out_dtype=F32, bias3=vec3(conv_pw1_b))
            u = _conv_ln(st, u, conv_dw, vec3(conv_dw_b), vec3(conv_ln_g), vec3(conv_ln_b), j)
            y = _mm_resid(st, u, conv_pw2, j, y, mods, layer, 2, tm=tm, tn=_tile(d, 512, LANES),
                          bias3=vec3(conv_pw2_b))
        else:
            fm = _fourier(st, h)
            y = _mm_resid(st, fm, fnet_w, j, y, mods, layer, 2, tm=tm, tn=_tile(d, 512, LANES),
                          bias3=vec3(fnet_b))

        f = layer // 2
        if layer % 2 == 0:
            h = _norm_mod(st, y, norm_g4, mods, layer, 1, BF16)
            dff = ffn_w1.shape[2]
            g = _mm_glu(h, ffn_w1, ffn_w3, f, n_out=dff, b_col_off=0, tm=_tile(st.n_tok, 2 * tm, 16),
                        tn=_tile(dff, 256, LANES), act="swiglu", out_dtype=BF16, x_buffers=1)
            y = _mm_resid(st, g, ffn_w2_b, f, y, mods, layer, 5, tm=_tile(tm, 512, 16),
                          tn=_tile(d, 512, LANES))
        else:
            rw = jnp.pad(router_w[f], ((0, 0), (0, ROUTE_LANES - n_experts)))
            h, route = _norm_mod(st, y, norm_g4, mods, layer, 1, F32, router_w_pad=rw,
                                 n_experts=n_experts)
            tme = _tile(tm, 512, 16)
            pos, src, tile_e, tile_valid = _moe_plan(route, n_experts, tme)
            xs = _gather_rows(h, src, _tile(tme, 256, 16))
            ye = _moe_experts(xs, tile_e, tile_valid, moe_w1, moe_w3, moe_w2, f, tme)
            outs = _moe_combine(st, y, ye, pos, route, mods, layer, 5, split=layer == depth - 1)
            y = outs[0] if len(outs) == 1 else outs

    k_ctx = jnp.stack(k_list, axis=1)
    v_ctx = jnp.stack(v_list, axis=1)
    yp, ys = y if isinstance(y, (list, tuple)) else (y[:st.n_ctx], y[st.n_ctx:])
    return (yp.reshape(batch, seq, d), ys.reshape(dec_batch, dec_seq, d), k_ctx, v_ctx)
```

```python
import functools
import math

import jax
import jax.numpy as jnp
from jax import lax
from jax.experimental import pallas as pl
from jax.experimental.pallas import tpu as pltpu

F32 = jnp.float32
BF16 = jnp.bfloat16

EPS = 1e-6
ROPE_THETA = 10000.0
GRID_W = 64
N_MIXERS = 3
N_FOURIER_GROUPS = 8
TOP_K = 2
LANES = 128
ROUTE_LANES = 128
VMEM_LIMIT_MB = 60


def _tile(n, pref, mult):
    t = min(pref, n)
    t -= t % mult
    while t >= mult:
        if n % t == 0:
            return t
        t -= mult
    return n


def _params(n_axes, vmem_mb=VMEM_LIMIT_MB):
    return pltpu.CompilerParams(dimension_semantics=("arbitrary",) * n_axes,
                                vmem_limit_bytes=vmem_mb * 1024 * 1024)


def _bdot(a, b):
    return jnp.dot(a.astype(BF16), b.astype(BF16), preferred_element_type=F32)


class _Stream:
    def __init__(self, batch, seq, dec_batch, dec_seq):
        self.batch, self.seq, self.dec_batch, self.dec_seq = batch, seq, dec_batch, dec_seq
        self.n_ctx = batch * seq
        self.n_lat = dec_batch * dec_seq
        self.n_tok = self.n_ctx + self.n_lat
        self.tm = _tile(math.gcd(self.n_ctx, dec_seq), 1024, 16)

    def mod_row(self, i, tm):
        r0 = i * tm
        return jnp.where(r0 < self.n_ctx, 0, 1 + (r0 - self.n_ctx) // self.dec_seq)

    def lat_tile(self, i, tm):
        t0 = self.n_ctx // tm
        return jnp.where(i >= t0, (i - t0) % (self.dec_seq // tm), 0)


def _mod_spec(st, layer, which, tm, width, col_axis):
    if col_axis:
        return pl.BlockSpec((None, None, None, 1, width),
                            lambda i, j: (layer, st.mod_row(i, tm), which, 0, j))
    return pl.BlockSpec((None, None, None, 1, width),
                        lambda i: (layer, st.mod_row(i, tm), which, 0, 0))


def _resid_specs(st, y, tm, tn, two_axes):
    if not isinstance(y, (tuple, list)):
        return [pl.BlockSpec((tm, tn), (lambda i, j: (i, j)) if two_axes else (lambda i: (i, 0)))], [y]
    t0 = st.n_ctx // tm
    if two_axes:
        maps = [lambda i, j: (jnp.minimum(i, t0 - 1), j), lambda i, j: (jnp.maximum(i - t0, 0), j)]
    else:
        maps = [lambda i: (jnp.minimum(i, t0 - 1), 0), lambda i: (jnp.maximum(i - t0, 0), 0)]
    return [pl.BlockSpec((tm, tn), m) for m in maps], list(y)


def _resid_tile(y_refs, ctx_tiles):
    if len(y_refs) == 1:
        return y_refs[0][...]
    return jnp.where(pl.program_id(0) < ctx_tiles, y_refs[0][...], y_refs[1][...])


def _adaln_kernel(c_ref, w_ref, b_ref, o_ref):
    c = c_ref[...]
    s = c * jax.nn.sigmoid(c)
    o_ref[...] = _bdot(s, w_ref[...]) + b_ref[...]


def _adaln_table(cvec8, mod_w, mod_b):
    depth, d, n6 = mod_w.shape
    tn = _tile(n6, 1024, LANES)
    out = pl.pallas_call(
        _adaln_kernel,
        grid=(depth, n6 // tn),
        in_specs=[pl.BlockSpec((8, d), lambda l, j: (0, 0)),
                  pl.BlockSpec((None, d, tn), lambda l, j: (l, 0, j)),
                  pl.BlockSpec((None, 1, tn), lambda l, j: (l, 0, j))],
        out_specs=pl.BlockSpec((None, 8, tn), lambda l, j: (l, 0, j)),
        out_shape=jax.ShapeDtypeStruct((depth, 8, n6), F32),
        compiler_params=_params(2),
        name="adaln_table",
    )(cvec8, mod_w, mod_b.reshape(depth, 1, n6))
    return out.reshape(depth, 8, 6, 1, d)


def _norm_mod_kernel(*refs, ctx_tiles):
    g_ref, sh_ref, sc_ref, o_ref = refs[-4:]
    y = _resid_tile(refs[:-4], ctx_tiles)
    h = y * lax.rsqrt(jnp.mean(y * y, axis=-1, keepdims=True) + EPS) * g_ref[...]
    o_ref[...] = (h * (1.0 + sc_ref[...]) + sh_ref[...]).astype(o_ref.dtype)


def _norm_mod_route_kernel(y_ref, g_ref, sh_ref, sc_ref, rw_ref, o_ref, r_ref, *, n_experts):
    y = y_ref[...]
    h = y * lax.rsqrt(jnp.mean(y * y, axis=-1, keepdims=True) + EPS) * g_ref[...]
    h = h * (1.0 + sc_ref[...]) + sh_ref[...]
    o_ref[...] = h
    logits = jnp.dot(h, rw_ref[...], preferred_element_type=F32,
                     precision=lax.Precision.HIGHEST)
    lane = lax.broadcasted_iota(jnp.int32, logits.shape, 1).astype(F32)
    neg = jnp.float32(-jnp.inf)
    lg = jnp.where(lane < n_experts, logits, neg)
    m1 = jnp.max(lg, axis=-1, keepdims=True)
    i1 = jnp.min(jnp.where(lg == m1, lane, float(ROUTE_LANES)), axis=-1, keepdims=True)
    lg2 = jnp.where(lane == i1, neg, lg)
    m2 = jnp.max(lg2, axis=-1, keepdims=True)
    i2 = jnp.min(jnp.where(lg2 == m2, lane, float(ROUTE_LANES)), axis=-1, keepdims=True)
    e = jnp.exp(m2 - m1)
    g1 = 1.0 / (1.0 + e)
    g2 = e / (1.0 + e)
    r_ref[...] = jnp.where(lane == 0, i1, jnp.where(lane == 1, i2,
                           jnp.where(lane == 2, g1, jnp.where(lane == 3, g2, 0.0))))


def _norm_mod(st, y, norm_g4, mods, layer, sub, out_dtype, router_w_pad=None, n_experts=0):
    n, d = st.n_tok, norm_g4.shape[-1]
    tm = _tile(st.tm, 512, 16)
    y_specs, y_args = _resid_specs(st, y, tm, d, False)
    in_specs = y_specs + [pl.BlockSpec((None, None, 1, d), lambda i: (layer, sub, 0, 0)),
                          _mod_spec(st, layer, 3 * sub + 0, tm, d, False),
                          _mod_spec(st, layer, 3 * sub + 1, tm, d, False)]
    if router_w_pad is None:
        return pl.pallas_call(
            functools.partial(_norm_mod_kernel, ctx_tiles=st.n_ctx // tm),
            grid=(n // tm,), in_specs=in_specs,
            out_specs=pl.BlockSpec((tm, d), lambda i: (i, 0)),
            out_shape=jax.ShapeDtypeStruct((n, d), out_dtype),
            compiler_params=_params(1), name="norm_mod",
        )(*y_args, norm_g4, mods, mods)
    (y,) = y_args
    in_specs.append(pl.BlockSpec((d, ROUTE_LANES), lambda i: (0, 0)))
    return pl.pallas_call(
        functools.partial(_norm_mod_route_kernel, n_experts=n_experts),
        grid=(n // tm,), in_specs=in_specs,
        out_specs=[pl.BlockSpec((tm, d), lambda i: (i, 0)),
                   pl.BlockSpec((tm, ROUTE_LANES), lambda i: (i, 0))],
        out_shape=[jax.ShapeDtypeStruct((n, d), F32),
                   jax.ShapeDtypeStruct((n, ROUTE_LANES), F32)],
        compiler_params=_params(1), name="norm_mod_route",
    )(y, norm_g4, mods, mods, router_w_pad)


def _mm_plain_kernel(x_ref, w_ref, *rest, has_bias):
    acc = _bdot(x_ref[...], w_ref[...])
    outs = rest
    if has_bias:
        acc = acc + rest[0][...]
        outs = rest[1:]
    for o_ref in outs:
        o_ref[...] = acc.astype(o_ref.dtype)


def _mm_plain(x, w3, wi, *, tm, tn, out_dtypes, bias3=None, block_diag=False):
    m = x.shape[0]
    kb, n_w = w3.shape[1], w3.shape[2]
    if block_diag:
        n_out, tn = x.shape[1], n_w
        x_spec = pl.BlockSpec((tm, kb), lambda i, j: (i, j))
        w_spec = pl.BlockSpec((None, kb, n_w), lambda i, j: (wi, 0, 0))
    else:
        n_out = n_w
        x_spec = pl.BlockSpec((tm, kb), lambda i, j: (i, 0))
        w_spec = pl.BlockSpec((None, kb, tn), lambda i, j: (wi, 0, j))
    in_specs, args = [x_spec, w_spec], [x, w3]
    if bias3 is not None:
        in_specs.append(pl.BlockSpec((None, 1, tn), lambda i, j: (wi, 0, j)))
        args.append(bias3)
    outs = pl.pallas_call(
        functools.partial(_mm_plain_kernel, has_bias=bias3 is not None),
        grid=(m // tm, n_out // tn), in_specs=in_specs,
        out_specs=[pl.BlockSpec((tm, tn), lambda i, j: (i, j)) for _ in out_dtypes],
        out_shape=[jax.ShapeDtypeStruct((m, n_out), dt) for dt in out_dtypes],
        compiler_params=_params(2), name="mm_plain",
    )(*args)
    return outs


def _mm_glu_kernel(x_ref, wa_ref, wb_ref, *rest, has_bias, act):
    x = x_ref[...]
    a = _bdot(x, wa_ref[...])
    b = _bdot(x, wb_ref[...])
    if has_bias:
        a = a + rest[0][...]
        b = b + rest[1][...]
    o_ref = rest[-1]
    if act == "swiglu":
        o_ref[...] = (a * jax.nn.sigmoid(a) * b).astype(o_ref.dtype)
    else:
        o_ref[...] = (a * jax.nn.sigmoid(b)).astype(o_ref.dtype)


def _mm_glu(x, wa3, wb3, wi, *, n_out, b_col_off, tm, tn, act, out_dtype, bias3=None, x_buffers=2):
    m, k = x.shape
    off = b_col_off // tn
    x_mode = {} if x_buffers == 2 else dict(pipeline_mode=pl.Buffered(x_buffers))
    in_specs = [pl.BlockSpec((tm, k), lambda i, j: (i, 0), **x_mode),
                pl.BlockSpec((None, k, tn), lambda i, j: (wi, 0, j)),
                pl.BlockSpec((None, k, tn), lambda i, j: (wi, 0, j + off))]
    args = [x, wa3, wb3]
    if bias3 is not None:
        in_specs += [pl.BlockSpec((None, 1, tn), lambda i, j: (wi, 0, j)),
                     pl.BlockSpec((None, 1, tn), lambda i, j: (wi, 0, j + off))]
        args += [bias3, bias3]
    return pl.pallas_call(
        functools.partial(_mm_glu_kernel, has_bias=bias3 is not None, act=act),
        grid=(m // tm, n_out // tn), in_specs=in_specs,
        out_specs=pl.BlockSpec((tm, tn), lambda i, j: (i, j)),
        out_shape=jax.ShapeDtypeStruct((m, n_out), out_dtype),
        compiler_params=_params(2), name="mm_glu",
    )(*args)


def _mm_resid_kernel(x_ref, w_ref, *rest, has_bias, ctx_tiles):
    acc = _bdot(x_ref[...], w_ref[...])
    if has_bias:
        acc = acc + rest[0][...]
        rest = rest[1:]
    gate_ref, o_ref = rest[-2:]
    o_ref[...] = _resid_tile(rest[:-2], ctx_tiles) + gate_ref[...] * acc


def _mm_resid(st, x, w3, wi, y, mods, layer, which_gate, *, tm, tn, bias3=None, x_buffers=2):
    m, k = x.shape
    n = w3.shape[2]
    x_mode = {} if x_buffers == 2 else dict(pipeline_mode=pl.Buffered(x_buffers))
    in_specs = [pl.BlockSpec((tm, k), lambda i, j: (i, 0), **x_mode),
                pl.BlockSpec((None, k, tn), lambda i, j: (wi, 0, j))]
    args = [x, w3]
    if bias3 is not None:
        in_specs.append(pl.BlockSpec((None, 1, tn), lambda i, j: (wi, 0, j)))
        args.append(bias3)
    y_specs, y_args = _resid_specs(st, y, tm, tn, True)
    in_specs += y_specs + [_mod_spec(st, layer, which_gate, tm, tn, True)]
    args += y_args + [mods]
    return pl.pallas_call(
        functools.partial(_mm_resid_kernel, has_bias=bias3 is not None, ctx_tiles=st.n_ctx // tm),
        grid=(m // tm, n // tn), in_specs=in_specs,
        out_specs=pl.BlockSpec((tm, tn), lambda i, j: (i, j)),
        out_shape=jax.ShapeDtypeStruct((m, n), F32),
        compiler_params=_params(2), name="mm_resid",
    )(*args)


def _headnorm_step(x_ref, w_ref, g_ref, cos_ref, sin_ref, ob_ref, of_ref, acc_new, acc_old):
    tm, tn = ob_ref.shape
    lane = lax.broadcasted_iota(jnp.int32, (tm, LANES), 1)
    low = (lane % (LANES // 2)) < (LANES // 4)
    g = g_ref[...]
    c = cos_ref[...]
    s = sin_ref[...]
    for hh in range(tn // LANES):
        cols = slice(hh * LANES, (hh + 1) * LANES)
        a = acc_old[:, cols]
        r = a * lax.rsqrt(jnp.mean(a * a, axis=-1, keepdims=True) + EPS) * g
        if of_ref is not None:
            of_ref[:, cols] = r
        partner = jnp.where(low, pltpu.roll(r, LANES - LANES // 4, 1), pltpu.roll(r, LANES // 4, 1))
        ob_ref[:, cols] = (r * c + partner * s).astype(ob_ref.dtype)
    acc_new[...] = _bdot(x_ref[...], w_ref[...])


def _mm_headnorm_kernel(x_ref, w_ref, g_ref, cos_ref, sin_ref, ob_ref, *rest):
    of_ref = rest[0] if len(rest) == 3 else None
    acc0, acc1 = rest[-2:]
    t = pl.program_id(0)

    @pl.when(t == 0)
    def _():
        acc1[...] = jnp.zeros_like(acc1)

    step = functools.partial(_headnorm_step, x_ref, w_ref, g_ref, cos_ref, sin_ref, ob_ref, of_ref)
    pl.when(t % 2 == 0)(functools.partial(step, acc0, acc1))
    pl.when(t % 2 == 1)(functools.partial(step, acc1, acc0))


def _mm_headnorm(st, x, w3, wi, g3, cos, sin, *, tm, tn, want_f32):
    m, k = x.shape
    n = w3.shape[2]
    nj = n // tn
    n_tiles = (m // tm) * nj
    ident = st.dec_seq // tm
    cur = lambda t: jnp.minimum(t, n_tiles - 1)
    prv = lambda t: jnp.maximum(t - 1, 0)

    def tab(t):
        i = prv(t) // nj
        return (jnp.where(i >= st.n_ctx // tm, st.lat_tile(i, tm), ident), 0)

    out_idx = lambda t: (prv(t) // nj, prv(t) % nj)
    out_specs = [pl.BlockSpec((tm, tn), out_idx)]
    out_shape = [jax.ShapeDtypeStruct((m, n), BF16)]
    if want_f32:
        out_specs.append(pl.BlockSpec((tm, tn), out_idx))
        out_shape.append(jax.ShapeDtypeStruct((m, n), F32))
    return pl.pallas_call(
        _mm_headnorm_kernel,
        grid=(n_tiles + 1,),
        in_specs=[pl.BlockSpec((tm, k), lambda t: (cur(t) // nj, 0)),
                  pl.BlockSpec((None, k, tn), lambda t: (wi, 0, cur(t) % nj)),
                  pl.BlockSpec((None, 1, LANES), lambda t: (wi, 0, 0)),
                  pl.BlockSpec((tm, LANES), tab),
                  pl.BlockSpec((tm, LANES), tab)],
        out_specs=out_specs, out_shape=out_shape,
        scratch_shapes=[pltpu.VMEM((tm, tn), F32), pltpu.VMEM((tm, tn), F32)],
        compiler_params=_params(1), name="mm_headnorm",
    )(x, w3, g3, cos, sin)


def _rope_tables(dec_seq, head_dim, tm, gain):
    half = head_dim // 2
    t = jnp.arange(dec_seq)
    rows = (t // GRID_W).astype(F32)
    cols = (t % GRID_W).astype(F32)
    inv = ROPE_THETA ** (-jnp.arange(0, half, 2, dtype=F32) / half)
    ar = rows[:, None] * inv
    ac = cols[:, None] * inv
    cos = jnp.concatenate([jnp.cos(ar), jnp.cos(ar), jnp.cos(ac), jnp.cos(ac)], axis=-1)
    sin = jnp.concatenate([-jnp.sin(ar), jnp.sin(ar), -jnp.sin(ac), jnp.sin(ac)], axis=-1)
    cos = jnp.concatenate([cos, jnp.ones((tm, head_dim), F32)], axis=0)
    sin = jnp.concatenate([sin, jnp.zeros((tm, head_dim), F32)], axis=0)
    return cos * gain, sin * gain


ATTN_KEY_CHUNK = 1024


def _flash(q4, kvs):
    dn = (((1,), (1,)), ((), ()))
    m = l = acc = None
    for k, v in kvs:
        s = lax.dot_general(q4, k, dn, preferred_element_type=F32)
        mc = s.max(axis=-1, keepdims=True)
        if m is None:
            m_new = mc
            p = jnp.exp2(s - m_new)
            l = p.sum(axis=-1, keepdims=True)
            acc = jnp.dot(p.astype(BF16), v, preferred_element_type=F32)
        else:
            m_new = jnp.maximum(m, mc)
            alpha = jnp.exp2(m - m_new)
            p = jnp.exp2(s - m_new)
            l = alpha * l + p.sum(axis=-1, keepdims=True)
            acc = alpha * acc + jnp.dot(p.astype(BF16), v, preferred_element_type=F32)
        m = m_new
    return acc / l


def _attn_ctx_kernel(q_ref, k_ref, v_ref, o_ref, *, n_seq, n_kv, group, head_dim):
    @pl.when(pl.program_id(0) >= n_seq)
    def _():
        o_ref[...] = jnp.zeros_like(o_ref)

    pl.when(pl.program_id(0) < n_seq)(
        functools.partial(_attn_ctx_body, q_ref, k_ref, v_ref, o_ref, n_kv, group, head_dim))


def _attn_ctx_body(q_ref, k_ref, v_ref, o_ref, n_kv, group, head_dim):
    tq = q_ref.shape[0]
    for kv in range(n_kv):
        kh = k_ref[:, kv * head_dim:(kv + 1) * head_dim]
        vh = v_ref[:, kv * head_dim:(kv + 1) * head_dim]
        c0 = kv * group * head_dim
        q4 = jnp.concatenate([q_ref[:, c0 + g * head_dim:c0 + (g + 1) * head_dim]
                              for g in range(group)], axis=0)
        o = _flash(q4, [(kh, vh)])
        for g in range(group):
            o_ref[:, c0 + g * head_dim:c0 + (g + 1) * head_dim] = (
                o[g * tq:(g + 1) * tq].astype(o_ref.dtype))


def _attn_lat_kernel(q_ref, k_ref, v_ref, ck_ref, cv_ref, prev_ref, o_ref, *, group, head_dim, chunk):
    del prev_ref
    tq = q_ref.shape[0]
    q4 = jnp.concatenate([q_ref[:, g * head_dim:(g + 1) * head_dim] for g in range(group)], axis=0)
    kvs = [(ck_ref[...].astype(BF16), cv_ref[...].astype(BF16))]
    for c0 in range(0, k_ref.shape[0], chunk):
        kvs.append((k_ref[c0:c0 + chunk, :], v_ref[c0:c0 + chunk, :]))
    o = _flash(q4, kvs)
    for g in range(group):
        o_ref[:, g * head_dim:(g + 1) * head_dim] = o[g * tq:(g + 1) * tq].astype(o_ref.dtype)


def _attention(st, q, k, v, cache_k3, cache_v3, n_kv, head_dim):
    n, d = q.shape
    kvd = n_kv * head_dim
    group = d // kvd
    gw = group * head_dim
    ctx = pl.pallas_call(
        functools.partial(_attn_ctx_kernel, n_seq=st.batch, n_kv=n_kv, group=group, head_dim=head_dim),
        grid=(n // st.seq,),
        in_specs=[pl.BlockSpec((st.seq, d), lambda b: (jnp.minimum(b, st.batch - 1), 0)),
                  pl.BlockSpec((st.seq, kvd), lambda b: (jnp.minimum(b, st.batch - 1), 0)),
                  pl.BlockSpec((st.seq, kvd), lambda b: (jnp.minimum(b, st.batch - 1), 0))],
        out_specs=pl.BlockSpec((st.seq, d), lambda b: (b, 0)),
        out_shape=jax.ShapeDtypeStruct((n, d), BF16),
        compiler_params=_params(1), name="attn_ctx",
    )(q, k, v)
    tq = _tile(st.dec_seq, 256, 16)
    qt = st.dec_seq // tq
    q0 = st.n_ctx // tq
    s0 = st.n_ctx // st.dec_seq
    past = cache_k3.shape[1]
    chunk = _tile(st.dec_seq, ATTN_KEY_CHUNK, LANES)
    return pl.pallas_call(
        functools.partial(_attn_lat_kernel, group=group, head_dim=head_dim, chunk=chunk),
        grid=(st.dec_batch, n_kv, qt),
        in_specs=[pl.BlockSpec((tq, gw), lambda b, h, i: (q0 + b * qt + i, h)),
                  pl.BlockSpec((st.dec_seq, head_dim), lambda b, h, i: (s0 + b, h)),
                  pl.BlockSpec((st.dec_seq, head_dim), lambda b, h, i: (s0 + b, h)),
                  pl.BlockSpec((None, past, head_dim), lambda b, h, i: (b, 0, h)),
                  pl.BlockSpec((None, past, head_dim), lambda b, h, i: (b, 0, h)),
                  pl.BlockSpec(memory_space=pl.ANY)],
        out_specs=pl.BlockSpec((tq, gw), lambda b, h, i: (q0 + b * qt + i, h)),
        out_shape=jax.ShapeDtypeStruct((n, d), BF16),
        input_output_aliases={5: 0},
        compiler_params=_params(3), name="attn_lat",
    )(q, k, v, cache_k3, cache_v3, ctx)


CONV_HALO = 16
F32_SUBLANES = 8
LN_ROWS = 16


def _conv_ln_kernel(cur_ref, prev_ref, next_ref, dw_ref, dwb_ref, g_ref, b_ref, o_ref, xp_ref, cv_ref,
                    sh_ref, *, st, tc, width, cw):
    i = pl.program_id(0)
    r0 = i * tc
    in_ctx = r0 < st.n_ctx
    pos = jnp.where(in_ctx, r0 % st.seq, (r0 - st.n_ctx) % st.dec_seq)
    seq_len = jnp.where(in_ctx, st.seq, st.dec_seq)
    first = pos == 0
    last = pos + tc == seq_len
    d = cur_ref.shape[1]
    xp_ref[0:CONV_HALO, :] = jnp.where(first, 0.0, prev_ref[...])
    xp_ref[CONV_HALO:CONV_HALO + tc, :] = cur_ref[...]
    xp_ref[CONV_HALO + tc:2 * CONV_HALO + tc, :] = jnp.where(last, 0.0, next_ref[...])
    base = CONV_HALO - width // 2

    n_a = -(-width // F32_SUBLANES)
    span = sh_ref.shape[1]

    def chunk(c, carry):
        off = pl.multiple_of(c * cw, cw)
        acc = jnp.zeros((tc, cw), F32)
        for b in range(F32_SUBLANES):
            start = base + b
            aligned = start % F32_SUBLANES == 0
            if not aligned:
                sh_ref[b] = xp_ref[pl.ds(start, span), pl.ds(off, cw)]
            for a in range(n_a):
                j = F32_SUBLANES * a + b
                if j >= width:
                    continue
                if aligned:
                    rows = xp_ref[pl.ds(start + F32_SUBLANES * a, tc), pl.ds(off, cw)]
                else:
                    rows = sh_ref[b, pl.ds(F32_SUBLANES * a, tc), :]
                acc = acc + rows * dw_ref[pl.ds(j, 1), pl.ds(off, cw)]
        cv_ref[:, pl.ds(off, cw)] = acc + dwb_ref[:, pl.ds(off, cw)]
        return carry

    lax.fori_loop(0, d // cw, chunk, 0)

    def ln_rows(rb, carry):
        rows = pl.ds(pl.multiple_of(rb * LN_ROWS, LN_ROWS), LN_ROWS)
        u = cv_ref[rows, :]
        mu = jnp.mean(u, axis=-1, keepdims=True)
        uc = u - mu
        var = jnp.mean(uc * uc, axis=-1, keepdims=True)
        yv = uc * lax.rsqrt(var + EPS) * g_ref[...] + b_ref[...]
        o_ref[rows, :] = (yv * jax.nn.sigmoid(yv)).astype(o_ref.dtype)
        return carry

    lax.fori_loop(0, tc // LN_ROWS, ln_rows, 0, unroll=4)


def _conv_ln(st, u, dw3, dwb3, lng3, lnb3, wi):
    n, d = u.shape
    width = dw3.shape[1]
    tc = _tile(math.gcd(st.seq, st.dec_seq), 256, CONV_HALO)
    hb = tc // CONV_HALO
    n_halo = n // CONV_HALO
    cw = LANES
    span = tc + F32_SUBLANES * (-(-width // F32_SUBLANES) - 1)
    vec = lambda: pl.BlockSpec((None, 1, d), lambda i: (wi, 0, 0))
    return pl.pallas_call(
        functools.partial(_conv_ln_kernel, st=st, tc=tc, width=width, cw=cw),
        grid=(n // tc,),
        in_specs=[pl.BlockSpec((tc, d), lambda i: (i, 0)),
                  pl.BlockSpec((CONV_HALO, d), lambda i: (jnp.maximum(i * hb - 1, 0), 0)),
                  pl.BlockSpec((CONV_HALO, d), lambda i: (jnp.minimum((i + 1) * hb, n_halo - 1), 0)),
                  pl.BlockSpec((None, width, d), lambda i: (wi, 0, 0)),
                  vec(), vec(), vec()],
        out_specs=pl.BlockSpec((tc, d), lambda i: (i, 0)),
        out_shape=jax.ShapeDtypeStruct((n, d), BF16),
        scratch_shapes=[pltpu.VMEM((tc + 2 * CONV_HALO, d), F32), pltpu.VMEM((tc, d), F32),
                        pltpu.VMEM((F32_SUBLANES, span, cw), F32)],
        compiler_params=_params(1), name="conv_ln",
    )(u, u, u, dw3, dwb3, lng3, lnb3)


def _dft_mats(n):
    nb = 1 << (n.bit_length() // 2)
    na = n // nb
    k = jnp.arange(n, dtype=jnp.int32)

    def thin(step, rows):
        ang = ((jnp.arange(rows, dtype=jnp.int32)[:, None] * step * k[None, :]) % n).astype(F32)
        ang = ang * (2.0 * math.pi / n)
        return jnp.cos(ang), jnp.sin(ang)

    ca, sa = (t[:, None, :] for t in thin(nb, na))
    cb, sb = (t[None, :, :] for t in thin(1, nb))
    cos = (ca * cb - sa * sb).reshape(n, n)
    sin = (sa * cb + ca * sb).reshape(n, n)
    return cos.astype(BF16), sin.astype(BF16)


def _dft_time_kernel(ct_ref, st_ref, xc_ref, xs_ref, *rest, scale, n_seq):
    o_ref = rest[-1]

    @pl.when(pl.program_id(1) < n_seq)
    def _():
        acc = (jnp.dot(ct_ref[...], xc_ref[...], preferred_element_type=F32)
               - jnp.dot(st_ref[...], xs_ref[...], preferred_element_type=F32))
        o_ref[...] = (acc * scale).astype(o_ref.dtype)

    @pl.when(pl.program_id(1) >= n_seq)
    def _():
        o_ref[...] = jnp.zeros_like(o_ref)


def _dft_time(xc, xs, n_seq, t, row0, scale, prev=None):
    n, d = xc.shape
    ct, sn = _dft_mats(t)
    tr = _tile(t, 512, 16)
    tn = _tile(d, 512 if t > 1024 else 2048, LANES)
    s0 = row0 // t
    r0 = row0 // tr
    rt = t // tr
    n_steps = n_seq if prev is not None else (n - row0) // t
    in_specs = [pl.BlockSpec((tr, t), lambda i, b, j: (i, 0)),
                pl.BlockSpec((tr, t), lambda i, b, j: (i, 0)),
                pl.BlockSpec((t, tn), lambda i, b, j: (s0 + jnp.minimum(b, n_seq - 1), j)),
                pl.BlockSpec((t, tn), lambda i, b, j: (s0 + jnp.minimum(b, n_seq - 1), j))]
    args = [ct, sn, xc, xs]
    aliases = {}
    if prev is not None:
        in_specs.append(pl.BlockSpec(memory_space=pl.ANY))
        args.append(prev)
        aliases = {4: 0}
    return pl.pallas_call(
        functools.partial(_dft_time_kernel, scale=scale, n_seq=n_seq),
        grid=(rt, n_steps, d // tn), in_specs=in_specs,
        out_specs=pl.BlockSpec((tr, tn), lambda i, b, j: (r0 + b * rt + i, j)),
        out_shape=jax.ShapeDtypeStruct((n, d), BF16),
        input_output_aliases=aliases,
        compiler_params=_params(3), name="dft_time",
    )(*args)


def _fourier(st, h):
    n, d = h.shape
    gsz = d // N_FOURIER_GROUPS
    cc, sc = _dft_mats(gsz)
    tm = _tile(n, 2048, 16)
    xc, = _mm_plain(h, cc[None], 0, tm=tm, tn=gsz, out_dtypes=[BF16], block_diag=True)
    xs, = _mm_plain(h, sc[None], 0, tm=tm, tn=gsz, out_dtypes=[BF16], block_diag=True)
    f = _dft_time(xc, xs, st.batch, st.seq, 0, (st.seq * gsz) ** -0.5)
    return _dft_time(xc, xs, st.dec_batch, st.dec_seq, st.n_ctx, (st.dec_seq * gsz) ** -0.5, prev=f)


DMA_ISSUE_UNROLL = 8
DMA_PRIORITIES = 2


def _gather_rows_kernel(src_ref, nxt_ref, h_ref, o_ref, buf_ref, sem):
    tg = buf_ref.shape[1]
    i = pl.program_id(0)
    slot = i % 2

    def issue_rows(idx_ref, dst_slot):
        def body(rp, carry):
            for prio in range(DMA_PRIORITIES):
                r = DMA_PRIORITIES * rp + prio
                pltpu.make_async_copy(h_ref.at[pl.ds(idx_ref[0, r], 1)], buf_ref.at[dst_slot, pl.ds(r, 1)],
                                      sem.at[dst_slot]).start(priority=prio)
            return carry
        lax.fori_loop(0, tg // DMA_PRIORITIES, body, 0, unroll=DMA_ISSUE_UNROLL)

    @pl.when(i == 0)
    def _():
        issue_rows(src_ref, 0)

    @pl.when(i + 1 < pl.num_programs(0))
    def _():
        issue_rows(nxt_ref, 1 - slot)

    pltpu.make_async_copy(h_ref.at[pl.ds(0, tg)], buf_ref.at[slot], sem.at[slot]).wait()
    o_ref[...] = buf_ref[slot].astype(o_ref.dtype)


def _gather_rows(h, src, tg):
    n, d = h.shape
    p = src.shape[0]
    steps = p // tg
    src3 = src.reshape(steps, 1, tg)
    return pl.pallas_call(
        _gather_rows_kernel,
        grid=(steps,),
        in_specs=[pl.BlockSpec((None, 1, tg), lambda i: (i, 0, 0), memory_space=pltpu.SMEM),
                  pl.BlockSpec((None, 1, tg), lambda i: (jnp.minimum(i + 1, steps - 1), 0, 0),
                               memory_space=pltpu.SMEM),
                  pl.BlockSpec(memory_space=pl.ANY)],
        out_specs=pl.BlockSpec((tg, d), lambda i: (i, 0)),
        out_shape=jax.ShapeDtypeStruct((p, d), BF16),
        scratch_shapes=[pltpu.VMEM((2, tg, d), F32), pltpu.SemaphoreType.DMA((2,))],
        compiler_params=_params(1), name="moe_gather",
    )(src3, src3, h)


def _moe_up_kernel(te_ref, tv_ref, x_ref, w1_ref, w3_ref, o_ref):
    i = pl.program_id(1)

    @pl.when(tv_ref[i] == 1)
    def _():
        x = x_ref[...]
        a = _bdot(x, w1_ref[...])
        b = _bdot(x, w3_ref[...])
        o_ref[...] = (a * jax.nn.sigmoid(a) * b).astype(o_ref.dtype)

    @pl.when(tv_ref[i] == 0)
    def _():
        o_ref[...] = jnp.zeros_like(o_ref)


def _moe_down_kernel(te_ref, tv_ref, x_ref, w_ref, o_ref):
    i = pl.program_id(1)

    @pl.when(tv_ref[i] == 1)
    def _():
        o_ref[...] = _bdot(x_ref[...], w_ref[...])

    @pl.when(tv_ref[i] == 0)
    def _():
        o_ref[...] = jnp.zeros_like(o_ref)


def _moe_experts(xs, tile_e, tile_valid, w1, w3, w2, f, tm):
    p, d = xs.shape
    ff = w1.shape[3]
    tn = _tile(ff, 512, LANES)
    g = pl.pallas_call(
        _moe_up_kernel,
        grid_spec=pltpu.PrefetchScalarGridSpec(
            num_scalar_prefetch=2, grid=(ff // tn, p // tm),
            in_specs=[pl.BlockSpec((tm, d), lambda j, i, te, tv: (i, 0)),
                      pl.BlockSpec((None, None, d, tn), lambda j, i, te, tv: (f, te[i], 0, j)),
                      pl.BlockSpec((None, None, d, tn), lambda j, i, te, tv: (f, te[i], 0, j))],
            out_specs=pl.BlockSpec((tm, tn), lambda j, i, te, tv: (i, j))),
        out_shape=jax.ShapeDtypeStruct((p, ff), BF16),
        compiler_params=_params(2), name="moe_up",
    )(tile_e, tile_valid, xs, w1, w3)
    tn2 = _tile(d, 1024, LANES)
    return pl.pallas_call(
        _moe_down_kernel,
        grid_spec=pltpu.PrefetchScalarGridSpec(
            num_scalar_prefetch=2, grid=(d // tn2, p // tm),
            in_specs=[pl.BlockSpec((tm, ff), lambda j, i, te, tv: (i, 0)),
                      pl.BlockSpec((None, None, ff, tn2), lambda j, i, te, tv: (f, te[i], 0, j))],
            out_specs=pl.BlockSpec((tm, tn2), lambda j, i, te, tv: (i, j))),
        out_shape=jax.ShapeDtypeStruct((p, d), F32),
        compiler_params=_params(2), name="moe_down",
    )(tile_e, tile_valid, g, w2)


def _moe_combine_kernel(pos_ref, nxt_ref, route_ref, y_ref, gate_ref, ye_ref, *rest, n_ctx_tiles):
    outs, (buf_ref, sem) = rest[:-2], rest[-2:]
    tc = buf_ref.shape[2]
    i = pl.program_id(0)
    slot = i % 2

    def issue_rows(idx_ref, dst_slot):
        def body(r, carry):
            for k in range(TOP_K):
                pltpu.make_async_copy(ye_ref.at[pl.ds(idx_ref[0, TOP_K * r + k], 1)],
                                      buf_ref.at[dst_slot, k, pl.ds(r, 1)],
                                      sem.at[dst_slot]).start(priority=k % DMA_PRIORITIES)
            return carry
        lax.fori_loop(0, tc, body, 0, unroll=DMA_ISSUE_UNROLL)

    @pl.when(i == 0)
    def _():
        issue_rows(pos_ref, 0)

    @pl.when(i + 1 < pl.num_programs(0))
    def _():
        issue_rows(nxt_ref, 1 - slot)

    for k in range(TOP_K):
        pltpu.make_async_copy(ye_ref.at[pl.ds(0, tc)], buf_ref.at[slot, k], sem.at[slot]).wait()
    route = route_ref[...]
    f = route[:, 2:3] * buf_ref[slot, 0] + route[:, 3:4] * buf_ref[slot, 1]
    res = y_ref[...] + gate_ref[...] * f
    if len(outs) == 1:
        outs[0][...] = res
    else:
        @pl.when(i < n_ctx_tiles)
        def _():
            outs[0][...] = res

        @pl.when(i >= n_ctx_tiles)
        def _():
            outs[1][...] = res


def _moe_combine(st, y, ye, pos, route, mods, layer, which_gate, split):
    n, d = y.shape
    tc = _tile(st.tm, 128, 8)
    steps = n // tc
    n_ctx_tiles = st.n_ctx // tc
    pos3 = pos.reshape(steps, 1, TOP_K * tc)
    if split:
        out_specs = [pl.BlockSpec((tc, d), lambda i: (jnp.minimum(i, n_ctx_tiles - 1), 0)),
                     pl.BlockSpec((tc, d), lambda i: (jnp.maximum(i - n_ctx_tiles, 0), 0))]
        out_shape = [jax.ShapeDtypeStruct((st.n_ctx, d), F32), jax.ShapeDtypeStruct((st.n_lat, d), F32)]
    else:
        out_specs = [pl.BlockSpec((tc, d), lambda i: (i, 0))]
        out_shape = [jax.ShapeDtypeStruct((n, d), F32)]
    return pl.pallas_call(
        functools.partial(_moe_combine_kernel, n_ctx_tiles=n_ctx_tiles),
        grid=(steps,),
        in_specs=[pl.BlockSpec((None, 1, TOP_K * tc), lambda i: (i, 0, 0), memory_space=pltpu.SMEM),
                  pl.BlockSpec((None, 1, TOP_K * tc), lambda i: (jnp.minimum(i + 1, steps - 1), 0, 0),
                               memory_space=pltpu.SMEM),
                  pl.BlockSpec((tc, ROUTE_LANES), lambda i: (i, 0)),
                  pl.BlockSpec((tc, d), lambda i: (i, 0)),
                  _mod_spec(st, layer, which_gate, tc, d, False),
                  pl.BlockSpec(memory_space=pl.ANY)],
        out_specs=out_specs, out_shape=out_shape,
        scratch_shapes=[pltpu.VMEM((2, TOP_K, tc, d), F32), pltpu.SemaphoreType.DMA((2,))],
        compiler_params=_params(1), name="moe_combine",
    )(pos3, pos3, route, y, mods, ye)


def _moe_plan(route, n_experts, tm):
    n = route.shape[0]
    e_flat = route[:, :TOP_K].astype(jnp.int32).reshape(-1)
    onehot = (e_flat[:, None] == jnp.arange(n_experts, dtype=jnp.int32)[None, :]).astype(jnp.int32)
    csum = jnp.cumsum(onehot, axis=0)
    rank = jnp.sum(onehot * csum, axis=1) - 1
    counts = csum[-1]
    padded = ((counts + tm - 1) // tm) * tm
    ends = jnp.cumsum(padded)
    starts = ends - padded
    pos = jnp.sum(onehot * starts[None, :], axis=1) + rank
    p_rows = TOP_K * n + n_experts * tm
    src = jnp.zeros((p_rows,), jnp.int32).at[pos].set(jnp.arange(TOP_K * n, dtype=jnp.int32) // TOP_K)
    tile_start = jnp.arange(p_rows // tm, dtype=jnp.int32) * tm
    tile_e = jnp.minimum(jnp.sum((tile_start[:, None] >= ends[None, :]).astype(jnp.int32), axis=1),
                         n_experts - 1)
    tile_valid = (tile_start < ends[-1]).astype(jnp.int32)
    return pos.reshape(n, TOP_K), src, tile_e, tile_valid


def kernel(x_prompt, x_sample, cache_k, cache_v, c, c_ctx, mod_w, mod_b, norm_g, wq, wk, wv, wo, q_norm, k_norm, conv_pw1, conv_pw1_b, conv_dw, conv_dw_b, conv_ln_g, conv_ln_b, conv_pw2, conv_pw2_b, fnet_w, fnet_b, ffn_w1, ffn_w3, ffn_w2, router_w, moe_w1, moe_w3, moe_w2):
    batch, seq, d = x_prompt.shape
    dec_batch, dec_seq, _ = x_sample.shape
    depth = mod_w.shape[0]
    n_attn, past, n_kv, head_dim = cache_k.shape[1:]
    n_experts = router_w.shape[2]
    st = _Stream(batch, seq, dec_batch, dec_seq)
    tm = st.tm
    assert st.n_ctx % dec_seq == 0 and dec_batch + 1 <= 8 and head_dim == LANES

    y = (x_prompt.reshape(st.n_ctx, d), x_sample.reshape(st.n_lat, d))
    cvec8 = jnp.concatenate([c_ctx[None, :], c, jnp.zeros((8 - 1 - dec_batch, d), F32)], axis=0)
    mods = _adaln_table(cvec8, mod_w, mod_b)
    norm_g4 = norm_g.reshape(depth, 2, 1, d)
    q_gain = head_dim ** -0.5 * math.log2(math.e)
    cos_q, sin_q = _rope_tables(dec_seq, head_dim, tm, q_gain)
    cos_k, sin_k = _rope_tables(dec_seq, head_dim, tm, 1.0)
    ffn_w2_b = ffn_w2.astype(BF16)
    wq, wk, wv, wo, conv_pw2, fnet_w = (w.astype(BF16) for w in (wq, wk, wv, wo, conv_pw2, fnet_w))

    def vec3(a):
        return a.reshape(a.shape[0], 1, a.shape[1])

    k_list, v_list = [], []
    for layer in range(depth):
        kind = layer % N_MIXERS
        j = layer // N_MIXERS
        h = _norm_mod(st, y, norm_g4, mods, layer, 0, BF16)
        if kind == 0:
            q, = _mm_headnorm(st, h, wq, j, vec3(q_norm), cos_q, sin_q, tm=tm, tn=_tile(d, 512, LANES),
                              want_f32=False)
            kb, kf = _mm_headnorm(st, h, wk, j, vec3(k_norm), cos_k, sin_k, tm=tm,
                                  tn=_tile(n_kv * head_dim, 512, LANES), want_f32=True)
            vb, vf = _mm_plain(h, wv, j, tm=tm, tn=_tile(n_kv * head_dim, 1024, LANES),
                               out_dtypes=[BF16, F32])
            k_list.append(kf[:st.n_ctx].reshape(batch, seq, n_kv, head_dim))
            v_list.append(vf[:st.n_ctx].reshape(batch, seq, n_kv, head_dim))
            a = _attention(st, q, kb, vb,
                           cache_k[:, j].reshape(dec_batch, past, n_kv * head_dim),
                           cache_v[:, j].reshape(dec_batch, past, n_kv * head_dim), n_kv, head_dim)
            y = _mm_resid(st, a, wo, j, y, mods, layer, 2, tm=tm, tn=_tile(d, 512, LANES))
        elif kind == 1:
            u = _mm_glu(h, conv_pw1, conv_pw1, j, n_out=d, b_col_off=d, tm=tm, tn=_tile(d, 256, LANES),
                        act="glu", out_dtype=F32, bias3=vec3(conv_pw1_b))
            u = _conv_ln(st, u, conv_dw, vec3(conv_dw_b), vec3(conv_ln_g), vec3(conv_ln_b), j)
            y = _mm_resid(st, u, conv_pw2, j, y, mods, layer, 2, tm=tm, tn=_tile(d, 512, LANES),
                          bias3=vec3(conv_pw2_b))
        else:
            fm = _fourier(st, h)
            y = _mm_resid(st, fm, fnet_w, j, y, mods, layer, 2, tm=tm, tn=_tile(d, 512, LANES),
                          bias3=vec3(fnet_b))

        f = layer // 2
        if layer % 2 == 0:
            h = _norm_mod(st, y, norm_g4, mods, layer, 1, BF16)
            dff = ffn_w1.shape[2]
            g = _mm_glu(h, ffn_w1, ffn_w3, f, n_out=dff, b_col_off=0, tm=_tile(st.n_tok, 2 * tm, 16),
                        tn=_tile(dff, 256, LANES), act="swiglu", out_dtype=BF16, x_buffers=1)
            y = _mm_resid(st, g, ffn_w2_b, f, y, mods, layer, 5, tm=tm, tn=_tile(d, 512, LANES),
                          x_buffers=1)
        else:
            rw = jnp.pad(router_w[f], ((0, 0), (0, ROUTE_LANES - n_experts)))
            h, route = _norm_mod(st, y, norm_g4, mods, layer, 1, F32, router_w_pad=rw,
                                 n_experts=n_experts)
            tme = _tile(tm, 512, 16)
            pos, src, tile_e, tile_valid = _moe_plan(route, n_experts, tme)
            xs = _gather_rows(h, src, _tile(tme, 256, 16))
            ye = _moe_experts(xs, tile_e, tile_valid, moe_w1, moe_w3, moe_w2, f, tme)
            outs = _moe_combine(st, y, ye, pos, route, mods, layer, 5, split=layer == depth - 1)
            y = outs[0] if len(outs) == 1 else outs

    k_ctx = jnp.stack(k_list, axis=1)
    v_ctx = jnp.stack(v_list, axis=1)
    yp, ys = y if isinstance(y, (list, tuple)) else (y[:st.n_ctx], y[st.n_ctx:])
    return (yp.reshape(batch, seq, d), ys.reshape(dec_batch, dec_seq, d), k_ctx, v_ctx)
```
